```python
import math
import jax
import jax.numpy as jnp
from jax import lax
import numpy as np

D_MODEL = 1024
BATCH = 8
SEQ = 2048
DEPTH = 4
DEC_BATCH = 128
DEC_SEQ = 8
PAST_LEN = 16384
PAGE_SIZE = 128

RMS_EPS = 1e-5
N_BRANCHES = 3
S5_WIDTH = D_MODEL
S5_GROUP = 16
S5_GROUPS = S5_WIDTH // S5_GROUP
S5_STATE = 64
LRU_WIDTH = D_MODEL
LRU_HEADS = 8
LRU_BLOCK = LRU_WIDTH // LRU_HEADS
CONV_WIDTH = 4
LRU_C = 8.0
RW_HEAD = 64
RW_WIDTH = D_MODEL
RW_HEADS = RW_WIDTH // RW_HEAD
RW_W_LORA = 64
RW_A_LORA = 64
RW_G_LORA = 160
RW_COLS = 3 * RW_WIDTH + RW_W_LORA + RW_A_LORA + RW_G_LORA
RW_LN_EPS = 64e-5
D_IN = S5_WIDTH + 2 * LRU_WIDTH + RW_COLS + N_BRANCHES * D_MODEL
N_EXPERTS = 32
TOP_K = 4
D_EXPERT = D_MODEL
SWIGLU_ALPHA = 1.702
SWIGLU_LIMIT = 7.0

kernel_name = "hybrid_s5_rglru_rwkv7_moe_step"


def _rmsnorm(x, g):
    xf = x.astype(jnp.float32)
    y = xf * lax.rsqrt(jnp.mean(xf * xf, axis=-1, keepdims=True) + RMS_EPS)
    return (y * g.astype(jnp.float32)).astype(x.dtype)


def _cmul(ar, ai, br, bi):
    return ar * br - ai * bi, ar * bi + ai * br


def _s5_combine(e1, e2):
    a1r, a1i, b1r, b1i = e1
    a2r, a2i, b2r, b2i = e2
    ar, ai = _cmul(a1r, a1i, a2r, a2i)
    br, bi = _cmul(a2r, a2i, b1r, b1i)
    return ar, ai, br + b2r, bi + b2i


def _lin_combine(e1, e2):
    a1, b1 = e1
    a2, b2 = e2
    return a1 * a2, a2 * b1 + b2


def _s5_mixer(u, h0_re, h0_im, lam_re, lam_im, log_step, b_re, b_im, c_re, c_im, d_skip, w_glu, b_glu):
    f32 = jnp.float32
    bsz, seq = u.shape[0], u.shape[1]
    uf = u.astype(f32).reshape(bsz, seq, S5_GROUPS, S5_GROUP)
    lr, li = lam_re.astype(f32), lam_im.astype(f32)
    step = jnp.exp(log_step.astype(f32))[:, None]
    mag = jnp.exp(lr * step)
    ab_re, ab_im = mag * jnp.cos(li * step), mag * jnp.sin(li * step)
    den = lr * lr + li * li
    q_re = ((ab_re - 1.0) * lr + ab_im * li) / den
    q_im = (ab_im * lr - (ab_re - 1.0) * li) / den
    bb_re, bb_im = _cmul(q_re[..., None], q_im[..., None], b_re.astype(f32), b_im.astype(f32))
    bu_re = jnp.einsum('blgp,gnp->blgn', uf, bb_re)
    bu_im = jnp.einsum('blgp,gnp->blgn', uf, bb_im)
    c0_re, c0_im = _cmul(ab_re, ab_im, h0_re.astype(f32), h0_im.astype(f32))
    bu_re = bu_re.at[:, 0].add(c0_re)
    bu_im = bu_im.at[:, 0].add(c0_im)
    a_re = jnp.broadcast_to(ab_re, (1, seq) + ab_re.shape)
    a_im = jnp.broadcast_to(ab_im, (1, seq) + ab_im.shape)
    _, _, h_re, h_im = lax.associative_scan(_s5_combine, (a_re, a_im, bu_re, bu_im), axis=1)
    y = (jnp.einsum('gpn,blgn->blgp', c_re.astype(f32), h_re)
         - jnp.einsum('gpn,blgn->blgp', c_im.astype(f32), h_im)
         + d_skip.astype(f32) * uf)
    y = jax.nn.gelu(y.reshape(bsz, seq, S5_WIDTH))
    y = y * jax.nn.sigmoid(y @ w_glu + b_glu)
    return y.astype(u.dtype), h_re[:, -1], h_im[:, -1]


def _rglru_mixer(xb, gb, h0, conv_buf, conv_w, conv_b, w_a, b_a, w_x, b_x, lam):
    f32 = jnp.float32
    bsz, seq = xb.shape[0], xb.shape[1]
    xp = jnp.concatenate([conv_buf.astype(f32), xb.astype(f32)], axis=1)
    cw = conv_w.astype(f32)
    xc = conv_b.astype(f32) + xp[:, 0:seq] * cw[0]
    for tap in range(1, CONV_WIDTH):
        xc = xc + xp[:, tap:tap + seq] * cw[tap]
    xh = xc.reshape(bsz, seq, LRU_HEADS, LRU_BLOCK)
    r = jax.nn.sigmoid(jnp.einsum('blhi,hij->blhj', xh, w_a.astype(f32)).reshape(bsz, seq, LRU_WIDTH) + b_a)
    i = jax.nn.sigmoid(jnp.einsum('blhi,hij->blhj', xh, w_x.astype(f32)).reshape(bsz, seq, LRU_WIDTH) + b_x)
    log_a = -LRU_C * r * jax.nn.softplus(-lam.astype(f32))
    a = jnp.exp(log_a)
    b = jnp.sqrt(-jnp.expm1(2.0 * log_a)) * (i * xc)
    b = b.at[:, 0].add(a[:, 0] * h0.astype(f32))
    _, h = lax.associative_scan(_lin_combine, (a, b), axis=1)
    y = h * jax.nn.gelu(gb.astype(f32))
    return y.astype(xb.dtype), h[:, -1], xp[:, -(CONV_WIDTH - 1):]


def _rwkv7_mixer(z, shift0, s0, mu, w0, w2, a0, a2, g2, k_k, k_a, r_k, ln_w, ln_b):
    f32 = jnp.float32
    bsz, seq = z.shape[0], z.shape[1]
    zf = z.astype(f32)
    zprev = jnp.concatenate([shift0.astype(f32)[:, None], zf[:, :-1]], axis=1)
    zs = zf + (zprev - zf) * mu
    o1 = 3 * RW_WIDTH + RW_W_LORA
    r, k, v, wl, al, gl = jnp.split(zs, [RW_WIDTH, 2 * RW_WIDTH, 3 * RW_WIDTH, o1, o1 + RW_A_LORA], axis=-1)
    w = -jax.nn.softplus(-(w0 + jnp.tanh(wl) @ w2)) - 0.5
    decay = jnp.exp(-jnp.exp(w))
    a = jax.nn.sigmoid(a0 + al @ a2)
    g = jax.nn.sigmoid(gl) @ g2

    def heads(t):
        return t.reshape(bsz, seq, RW_HEADS, RW_HEAD)

    kk = heads(k * k_k)
    kk = kk * lax.rsqrt(jnp.maximum(jnp.sum(kk * kk, axis=-1, keepdims=True), 1e-24))
    k = heads(k * (1.0 + (a - 1.0) * k_a))
    r, decay, v, a = heads(r), heads(decay), heads(v), heads(a)

    def step(S, inp):
        r_t, w_t, k_t, v_t, kk_t, a_t = inp
        sa = jnp.einsum('bhij,bhj->bhi', S, -kk_t)
        S = (S * w_t[:, :, None, :] + sa[..., None] * (kk_t * a_t)[:, :, None, :]
             + v_t[..., None] * k_t[:, :, None, :])
        return S, jnp.einsum('bhij,bhj->bhi', S, r_t)

    xs = tuple(jnp.swapaxes(t, 0, 1) for t in (r, decay, k, v, kk, a))
    s_fin, ys = lax.scan(step, s0.astype(f32), xs)
    ys = jnp.swapaxes(ys, 0, 1)
    mean = jnp.mean(ys, axis=-1, keepdims=True)
    var = jnp.mean(jnp.square(ys - mean), axis=-1, keepdims=True)
    yn = ((ys - mean) * lax.rsqrt(var + RW_LN_EPS)).reshape(bsz, seq, RW_WIDTH) * ln_w + ln_b
    bonus = (jnp.sum(r * k * r_k, axis=-1, keepdims=True) * v).reshape(bsz, seq, RW_WIDTH)
    out = (yn + bonus) * g
    return out.astype(z.dtype), s_fin, zf[:, -1]


def _moe(h, w_router, b_router, w_gu, b_gu, w_down, b_down):
    f32 = jnp.float32
    bsz, seq, d = h.shape
    t = h.reshape(bsz * seq, d)
    logits = (t @ w_router + b_router).astype(f32)
    top_v, top_i = lax.top_k(logits, TOP_K)
    wts = jax.nn.softmax(top_v, axis=-1)
    combine = jnp.einsum('tk,tke->te', wts, jax.nn.one_hot(top_i, N_EXPERTS, dtype=f32))
    y = jnp.zeros((bsz * seq, d), f32)
    for e in range(N_EXPERTS):
        gu = (t @ w_gu[e] + b_gu[e]).astype(f32)
        glu = jnp.minimum(gu[:, :D_EXPERT], SWIGLU_LIMIT)
        lin = jnp.clip(gu[:, D_EXPERT:], -SWIGLU_LIMIT, SWIGLU_LIMIT)
        act = glu * jax.nn.sigmoid(SWIGLU_ALPHA * glu) * (lin + 1.0)
        y = y + combine[:, e:e + 1] * (act.astype(h.dtype) @ w_down[e] + b_down[e])
    return y.astype(h.dtype).reshape(bsz, seq, d)


def _trunk_layer(x, c, st, p):
    (w_mod, b_mod, norm1_g, w_in, s5_lam_re, s5_lam_im, s5_log_step, s5_b_re, s5_b_im,
     s5_c_re, s5_c_im, s5_d, s5_w_glu, s5_b_glu, lru_conv_w, lru_conv_b, lru_w_a, lru_b_a,
     lru_w_x, lru_b_x, lru_lam, rw_mu, rw_w0, rw_w2, rw_a0, rw_a2, rw_g2, rw_k_k, rw_k_a,
     rw_r_k, rw_ln_w, rw_ln_b, w_br_s5, w_br_lru, w_br_rw, w_out, norm2_g,
     moe_w_router, moe_b_router, moe_w_gu, moe_b_gu, moe_w_down, moe_b_down) = p
    s5_re, s5_im, lru_h, lru_conv, rw_s, rw_shift = st
    mod = (jax.nn.silu(c) @ w_mod + b_mod)[:, None, :]
    sh1, sc1, gt1, sh2, sc2, gt2 = jnp.split(mod, 6, axis=-1)
    h = _rmsnorm(x, norm1_g) * (1.0 + sc1) + sh1
    z = h @ w_in
    o1 = S5_WIDTH
    o2 = o1 + LRU_WIDTH
    o3 = o2 + LRU_WIDTH
    o4 = o3 + RW_COLS
    u_s5, x_lru, g_lru, z_rw, z_gate = jnp.split(z, [o1, o2, o3, o4], axis=-1)
    y_s5, n_s5_re, n_s5_im = _s5_mixer(u_s5, s5_re, s5_im, s5_lam_re, s5_lam_im, s5_log_step,
                                       s5_b_re, s5_b_im, s5_c_re, s5_c_im, s5_d, s5_w_glu, s5_b_glu)
    y_lru, n_lru_h, n_lru_conv = _rglru_mixer(x_lru, g_lru, lru_h, lru_conv, lru_conv_w, lru_conv_b,
                                              lru_w_a, lru_b_a, lru_w_x, lru_b_x, lru_lam)
    y_rw, n_rw_s, n_rw_shift = _rwkv7_mixer(z_rw, rw_shift, rw_s, rw_mu, rw_w0, rw_w2, rw_a0, rw_a2,
                                            rw_g2, rw_k_k, rw_k_a, rw_r_k, rw_ln_w, rw_ln_b)
    gates = jax.nn.sigmoid(z_gate.astype(jnp.float32)).astype(x.dtype)
    gate_s5, gate_lru, gate_rw = jnp.split(gates, N_BRANCHES, axis=-1)
    merged = gate_s5 * (y_s5 @ w_br_s5) + gate_lru * (y_lru @ w_br_lru) + gate_rw * (y_rw @ w_br_rw)
    x = x + gt1 * (merged @ w_out)
    h2 = _rmsnorm(x, norm2_g) * (1.0 + sc2) + sh2
    x = x + gt2 * _moe(h2, moe_w_router, moe_b_router, moe_w_gu, moe_b_gu, moe_w_down, moe_b_down)
    return x, (n_s5_re, n_s5_im, n_lru_h, n_lru_conv, n_rw_s, n_rw_shift)


def _run_group(x, c, states, layer_params, final_g):
    collected = tuple([] for _ in states)
    for l in range(DEPTH):
        p = tuple(a[l] for a in layer_params)
        st = tuple(s[l] for s in states)
        x, new = _trunk_layer(x, c, st, p)
        for lst, s in zip(collected, new):
            lst.append(s.astype(x.dtype))
    return _rmsnorm(x, final_g), tuple(jnp.stack(lst) for lst in collected)


def setup_inputs(seed: int = 0) -> dict:
    key = jax.random.key(seed)
    keys = iter(jax.random.split(key, 64))
    f32 = jnp.float32

    def nrm(shape, scale):
        return jax.random.normal(next(keys), shape, f32) * scale

    def uni(shape, lo, hi):
        return jax.random.uniform(next(keys), shape, f32, lo, hi)

    D = D_MODEL
    inp = {}
    inp["x_prompt"] = nrm((BATCH, SEQ, D), 1.0)
    inp["x_sample"] = nrm((DEC_BATCH, DEC_SEQ, D), 1.0)
    inp["state_s5_re"] = nrm((DEPTH, DEC_BATCH, S5_GROUPS, S5_STATE), 0.1)
    inp["state_s5_im"] = nrm((DEPTH, DEC_BATCH, S5_GROUPS, S5_STATE), 0.1)
    inp["state_lru_h"] = nrm((DEPTH, DEC_BATCH, LRU_WIDTH), 0.5)
    inp["cache_lru_conv"] = nrm((DEPTH, DEC_BATCH, CONV_WIDTH - 1, LRU_WIDTH), 1.0)
    inp["state_rwkv"] = nrm((DEPTH, DEC_BATCH, RW_HEADS, RW_HEAD, RW_HEAD), 0.1)
    inp["cache_rwkv_shift"] = nrm((DEPTH, DEC_BATCH, RW_COLS), 1.0)
    inp["c_prompt"] = nrm((BATCH, D), 1.0)
    inp["c_sample"] = nrm((DEC_BATCH, D), 1.0)
    inp["w_mod"] = nrm((DEPTH, D, 6 * D), 0.5 * D ** -0.5)
    inp["b_mod"] = nrm((DEPTH, 6 * D), 0.02)
    inp["norm1_g"] = 1.0 + nrm((DEPTH, D), 0.02)
    inp["w_in"] = nrm((DEPTH, D, D_IN), D ** -0.5)
    inp["s5_lam_re"] = -0.5 + nrm((DEPTH, S5_GROUPS, S5_STATE), 0.01)
    inp["s5_lam_im"] = (math.pi * jnp.broadcast_to(jnp.arange(S5_STATE, dtype=f32), (DEPTH, S5_GROUPS, S5_STATE))
                         + nrm((DEPTH, S5_GROUPS, S5_STATE), 0.01))
    inp["s5_log_step"] = uni((DEPTH, S5_GROUPS), math.log(1e-3), math.log(1e-1))
    inp["s5_b_re"] = nrm((DEPTH, S5_GROUPS, S5_STATE, S5_GROUP), (2 * S5_GROUP) ** -0.5)
    inp["s5_b_im"] = nrm((DEPTH, S5_GROUPS, S5_STATE, S5_GROUP), (2 * S5_GROUP) ** -0.5)
    inp["s5_c_re"] = nrm((DEPTH, S5_GROUPS, S5_GROUP, S5_STATE), (0.5 * S5_STATE) ** -0.5)
    inp["s5_c_im"] = nrm((DEPTH, S5_GROUPS, S5_GROUP, S5_STATE), (0.5 * S5_STATE) ** -0.5)
    inp["s5_d"] = nrm((DEPTH, S5_GROUPS, S5_GROUP), 0.5)
    inp["s5_w_glu"] = nrm((DEPTH, S5_WIDTH, S5_WIDTH), S5_WIDTH ** -0.5)
    inp["s5_b_glu"] = nrm((DEPTH, S5_WIDTH), 0.02)
    inp["lru_conv_w"] = nrm((DEPTH, CONV_WIDTH, LRU_WIDTH), CONV_WIDTH ** -0.5)
    inp["lru_conv_b"] = nrm((DEPTH, LRU_WIDTH), 0.02)
    inp["lru_w_a"] = nrm((DEPTH, LRU_HEADS, LRU_BLOCK, LRU_BLOCK), LRU_BLOCK ** -0.5)
    inp["lru_b_a"] = nrm((DEPTH, LRU_WIDTH), 0.02)
    inp["lru_w_x"] = nrm((DEPTH, LRU_HEADS, LRU_BLOCK, LRU_BLOCK), LRU_BLOCK ** -0.5)
    inp["lru_b_x"] = nrm((DEPTH, LRU_WIDTH), 0.02)
    a_c = uni((DEPTH, LRU_WIDTH), 0.9, 0.999)
    s = a_c ** (1.0 / LRU_C)
    inp["lru_lam"] = jnp.log(s) - jnp.log1p(-s)
    inp["rw_mu"] = uni((DEPTH, RW_COLS), 0.0, 1.0)
    inp["rw_w0"] = uni((DEPTH, RW_WIDTH), -6.0, -1.0)
    inp["rw_w2"] = nrm((DEPTH, RW_W_LORA, RW_WIDTH), 0.5 * RW_W_LORA ** -0.5)
    inp["rw_a0"] = nrm((DEPTH, RW_WIDTH), 0.1)
    inp["rw_a2"] = nrm((DEPTH, RW_A_LORA, RW_WIDTH), 0.5 * RW_A_LORA ** -0.5)
    inp["rw_g2"] = nrm((DEPTH, RW_G_LORA, RW_WIDTH), RW_G_LORA ** -0.5)
    inp["rw_k_k"] = 0.85 + nrm((DEPTH, RW_WIDTH), 0.02)
    inp["rw_k_a"] = 1.0 + nrm((DEPTH, RW_WIDTH), 0.02)
    inp["rw_r_k"] = nrm((DEPTH, RW_HEADS, RW_HEAD), 0.1)
    inp["rw_ln_w"] = 1.0 + nrm((DEPTH, RW_WIDTH), 0.02)
    inp["rw_ln_b"] = nrm((DEPTH, RW_WIDTH), 0.02)
    inp["w_br_s5"] = nrm((DEPTH, S5_WIDTH, D), S5_WIDTH ** -0.5)
    inp["w_br_lru"] = nrm((DEPTH, LRU_WIDTH, D), LRU_WIDTH ** -0.5)
    inp["w_br_rw"] = nrm((DEPTH, RW_WIDTH, D), RW_WIDTH ** -0.5)
    inp["w_out"] = nrm((DEPTH, D, D), D ** -0.5)
    inp["norm2_g"] = 1.0 + nrm((DEPTH, D), 0.02)
    inp["moe_w_router"] = nrm((DEPTH, D, N_EXPERTS), D ** -0.5)
    inp["moe_b_router"] = nrm((DEPTH, N_EXPERTS), 0.01)
    inp["moe_w_gu"] = nrm((DEPTH, N_EXPERTS, D, 2 * D_EXPERT), D ** -0.5)
    inp["moe_b_gu"] = nrm((DEPTH, N_EXPERTS, 2 * D_EXPERT), 0.02)
    inp["moe_w_down"] = nrm((DEPTH, N_EXPERTS, D_EXPERT, D), D_EXPERT ** -0.5)
    inp["moe_b_down"] = nrm((DEPTH, N_EXPERTS, D), 0.02)
    inp["final_g"] = 1.0 + nrm((D,), 0.02)
    return inp


def reference(x_prompt, x_sample, state_s5_re, state_s5_im, state_lru_h, cache_lru_conv, state_rwkv,
              cache_rwkv_shift, c_prompt, c_sample, w_mod, b_mod, norm1_g, w_in, s5_lam_re, s5_lam_im,
              s5_log_step, s5_b_re, s5_b_im, s5_c_re, s5_c_im, s5_d, s5_w_glu, s5_b_glu, lru_conv_w,
              lru_conv_b, lru_w_a, lru_b_a, lru_w_x, lru_b_x, lru_lam, rw_mu, rw_w0, rw_w2, rw_a0, rw_a2,
              rw_g2, rw_k_k, rw_k_a, rw_r_k, rw_ln_w, rw_ln_b, w_br_s5, w_br_lru, w_br_rw, w_out, norm2_g,
              moe_w_router, moe_b_router, moe_w_gu, moe_b_gu, moe_w_down, moe_b_down, final_g):
    layer_params = (w_mod, b_mod, norm1_g, w_in, s5_lam_re, s5_lam_im, s5_log_step, s5_b_re, s5_b_im,
                    s5_c_re, s5_c_im, s5_d, s5_w_glu, s5_b_glu, lru_conv_w, lru_conv_b, lru_w_a, lru_b_a,
                    lru_w_x, lru_b_x, lru_lam, rw_mu, rw_w0, rw_w2, rw_a0, rw_a2, rw_g2, rw_k_k, rw_k_a,
                    rw_r_k, rw_ln_w, rw_ln_b, w_br_s5, w_br_lru, w_br_rw, w_out, norm2_g,
                    moe_w_router, moe_b_router, moe_w_gu, moe_b_gu, moe_w_down, moe_b_down)
    sample_states = (state_s5_re, state_s5_im, state_lru_h, cache_lru_conv, state_rwkv, cache_rwkv_shift)
    bp = x_prompt.shape[0]
    prompt_states = tuple(jnp.zeros((DEPTH, bp) + s.shape[2:], x_prompt.dtype) for s in sample_states)
    y_prompt, (p_s5_re, p_s5_im, p_lru_h, p_lru_conv, p_rwkv, p_rwkv_shift) = _run_group(
        x_prompt, c_prompt, prompt_states, layer_params, final_g)
    y_sample, (s_s5_re, s_s5_im, s_lru_h, s_lru_conv, s_rwkv, s_rwkv_shift) = _run_group(
        x_sample, c_sample, sample_states, layer_params, final_g)
    return (y_prompt, y_sample, p_s5_re, p_s5_im, p_lru_h, p_lru_conv, p_rwkv, p_rwkv_shift,
            s_s5_re, s_s5_im, s_lru_h, s_lru_conv, s_rwkv, s_rwkv_shift)
```

```python
import functools
import math

import jax
import jax.numpy as jnp
from jax import lax
from jax.experimental import pallas as pl
from jax.experimental.pallas import tpu as pltpu

F32 = jnp.float32
BF16 = jnp.bfloat16
HIGHEST = lax.Precision.HIGHEST

D_MODEL = 1024
RMS_EPS = 1e-5
S5_GROUP = 16
S5_GROUPS = D_MODEL // S5_GROUP
S5_STATE = 64
S5_LANES = S5_GROUPS * S5_STATE
LRU_HEADS = 8
LRU_BLOCK = D_MODEL // LRU_HEADS
CONV_WIDTH = 4
LRU_C = 8.0
RW_HEAD = 64
RW_HEADS = D_MODEL // RW_HEAD
RW_W_LORA = 64
RW_A_LORA = 64
RW_G_LORA = 160
RW_LORA = RW_W_LORA + RW_A_LORA + RW_G_LORA
RW_COLS = 3 * D_MODEL + RW_LORA
RW_LN_EPS = 64e-5
N_EXPERTS = 32
TOP_K = 4
SWIGLU_ALPHA = 1.702
SWIGLU_LIMIT = 7.0

LANES = 128
SUBLANES = 8
VMEM_LIMIT = 56 * 1024 * 1024

LORA_PAD = 512
COL_S5 = 0
COL_XLRU = 1 * D_MODEL
COL_GLRU = 2 * D_MODEL
COL_R = 3 * D_MODEL
COL_K = 4 * D_MODEL
COL_V = 5 * D_MODEL
COL_GATE = 6 * D_MODEL
COL_LORA = 9 * D_MODEL
D_IN_PAD = COL_LORA + LORA_PAD

S5_LANE_BLOCK = LANES
S5_STATE_BLOCK = (LANES // S5_GROUP) * S5_STATE
EXPERT_TILE = 512
ROUTE_TILE = 512


def _cp(sem, vmem=VMEM_LIMIT):
    return pltpu.CompilerParams(dimension_semantics=sem, vmem_limit_bytes=vmem)


def _gelu(x):
    return 0.5 * x * (1.0 + jnp.tanh(math.sqrt(2.0 / math.pi) * (x + 0.044715 * (x * x * x))))


def _sigmoid(x):
    return 1.0 / (1.0 + jnp.exp(-x))


def _softplus(x):
    return jnp.maximum(x, 0.0) + jnp.log1p(jnp.exp(-jnp.abs(x)))


def _tiles(seq, batch, tokens):
    bb = min(batch, max(SUBLANES, (tokens // seq) // SUBLANES * SUBLANES))
    while batch % bb:
        bb -= SUBLANES
    lt = max(1, min(seq, tokens // bb))
    while seq % lt:
        lt -= 1
    return lt, bb


def _mod_kernel(c_ref, w_ref, b_ref, o_ref):
    c = c_ref[...]
    s = c * _sigmoid(c)
    o_ref[0] = jnp.dot(s, w_ref[0], precision=HIGHEST, preferred_element_type=F32) + b_ref[0]


def _modulation(c, w_mod, b_mod):
    depth, d, n = w_mod.shape
    bc = c.shape[0]
    tn = 1536
    return pl.pallas_call(
        _mod_kernel,
        grid=(depth, n // tn),
        in_specs=[pl.BlockSpec((bc, d), lambda l, j: (0, 0)),
                  pl.BlockSpec((1, d, tn), lambda l, j: (l, 0, j)),
                  pl.BlockSpec((1, 1, tn), lambda l, j: (l, 0, j))],
        out_specs=pl.BlockSpec((1, bc, tn), lambda l, j: (l, 0, j)),
        out_shape=jax.ShapeDtypeStruct((depth, bc, n), F32),
        compiler_params=_cp(("arbitrary", "arbitrary")),
    )(c, w_mod, b_mod.reshape(depth, 1, n))


def _in_kernel(x_ref, sh_ref, sc_ref, g_ref, w_ref, o_ref, h_scr):
    lt, bb, d = x_ref.shape

    @pl.when(pl.program_id(2) == 0)
    def _():
        x = x_ref[...]
        y = x * lax.rsqrt(jnp.mean(x * x, axis=-1, keepdims=True) + RMS_EPS) * g_ref[...]
        h = y * (1.0 + sc_ref[...]) + sh_ref[...]
        h_scr[...] = h.reshape(lt * bb, d).astype(BF16)

    o = jnp.dot(h_scr[...], w_ref[...], preferred_element_type=F32)
    o_ref[...] = o.reshape(lt, bb, o.shape[-1])


def _in_proj(x, mod, norm_g, w_in_p, tokens=1024, tn=512):
    seq, batch, d = x.shape
    lt, bb = _tiles(seq, batch, tokens)
    n = w_in_p.shape[1]
    return pl.pallas_call(
        _in_kernel,
        grid=(batch // bb, seq // lt, n // tn),
        in_specs=[pl.BlockSpec((lt, bb, d), lambda b, t, j: (t, b, 0)),
                  pl.BlockSpec((1, bb, d), lambda b, t, j: (0, b, 0)),
                  pl.BlockSpec((1, bb, d), lambda b, t, j: (0, b, 1)),
                  pl.BlockSpec((1, 1, d), lambda b, t, j: (0, 0, 0)),
                  pl.BlockSpec((d, tn), lambda b, t, j: (0, j))],
        out_specs=pl.BlockSpec((lt, bb, tn), lambda b, t, j: (t, b, j)),
        out_shape=jax.ShapeDtypeStruct((seq, batch, n), F32),
        scratch_shapes=[pltpu.VMEM((lt * bb, d), BF16)],
        compiler_params=_cp(("arbitrary", "arbitrary", "arbitrary")),
    )(x, mod, mod, norm_g.reshape(1, 1, d), w_in_p)


def _s5_kernel(u_ref, bbr_ref, bbi_ref, ar_ref, ai_ref, ctr_ref, cti_ref, d_ref, h0r_ref, h0i_ref,
               y_ref, hr_ref, hi_ref, sr, si):
    lt, bb, nl = u_ref.shape
    ns = sr.shape[-1]

    @pl.when(pl.program_id(2) == 0)
    def _():
        hr_ref[...] = h0r_ref[...]
        hi_ref[...] = h0i_ref[...]

    u2 = u_ref[...].reshape(lt * bb, nl)
    ub = u2.astype(BF16)
    sr[...] = jnp.dot(ub, bbr_ref[0], preferred_element_type=F32).reshape(lt, bb, ns)
    si[...] = jnp.dot(ub, bbi_ref[0], preferred_element_type=F32).reshape(lt, bb, ns)
    ar = jnp.broadcast_to(ar_ref[...], (bb, ns))
    ai = jnp.broadcast_to(ai_ref[...], (bb, ns))

    def body(i, carry):
        hr, hi = carry
        nr = ar * hr - ai * hi + sr[i]
        ni = ar * hi + ai * hr + si[i]
        sr[i] = nr
        si[i] = ni
        return nr, ni

    hr, hi = lax.fori_loop(0, lt, body, (hr_ref[...], hi_ref[...]), unroll=min(lt, 8))
    hr_ref[...] = hr
    hi_ref[...] = hi
    y = (jnp.dot(sr[...].reshape(lt * bb, ns).astype(BF16), ctr_ref[0], preferred_element_type=F32)
         - jnp.dot(si[...].reshape(lt * bb, ns).astype(BF16), cti_ref[0], preferred_element_type=F32)
         + d_ref[...] * u2)
    y_ref[...] = y.reshape(lt, bb, nl)


def _s5_params_kernel(lr_ref, li_ref, ls_ref, abr_ref, abi_ref, qr_ref, qi_ref):
    lr = lr_ref[...]
    li = li_ref[...]
    step = jnp.exp(ls_ref[...])
    mag = jnp.exp(lr * step)
    ab_re = mag * jnp.cos(li * step)
    ab_im = mag * jnp.sin(li * step)
    den = lr * lr + li * li
    abr_ref[...] = ab_re
    abi_ref[...] = ab_im
    qr_ref[...] = ((ab_re - 1.0) * lr + ab_im * li) / den
    qi_ref[...] = (ab_im * lr - (ab_re - 1.0) * li) / den


def _s5_params(lam_re, lam_im, log_step):
    depth, g, n = lam_re.shape
    shp = jax.ShapeDtypeStruct((depth * g, n), F32)
    return pl.pallas_call(_s5_params_kernel, out_shape=(shp, shp, shp, shp))(
        lam_re.reshape(depth * g, n), lam_im.reshape(depth * g, n), log_step.reshape(depth * g, 1))


def _block_diag(blocks, per):
    g, a, b = blocks.shape
    x = blocks.reshape(g // per, per, a, b)
    eye = jnp.eye(per, dtype=blocks.dtype)
    return jnp.einsum("gpab,pq->gpaqb", x, eye).reshape(g // per, per * a, per * b)


def _s5_mixer(z, ab_re, ab_im, bb_re, bb_im, ct_re, ct_im, d_skip, h0_re, h0_im, tokens=512):
    seq, batch, _ = z.shape
    lt, bb = _tiles(seq, batch, tokens)
    nl, ns = S5_LANE_BLOCK, S5_STATE_BLOCK
    nblk = D_MODEL // nl
    col0 = COL_S5 // nl
    y, hr, hi = pl.pallas_call(
        _s5_kernel,
        grid=(batch // bb, nblk, seq // lt),
        in_specs=[pl.BlockSpec((lt, bb, nl), lambda b, j, t: (t, b, col0 + j)),
                  pl.BlockSpec((1, nl, ns), lambda b, j, t: (j, 0, 0)),
                  pl.BlockSpec((1, nl, ns), lambda b, j, t: (j, 0, 0)),
                  pl.BlockSpec((1, ns), lambda b, j, t: (0, j)),
                  pl.BlockSpec((1, ns), lambda b, j, t: (0, j)),
                  pl.BlockSpec((1, ns, nl), lambda b, j, t: (j, 0, 0)),
                  pl.BlockSpec((1, ns, nl), lambda b, j, t: (j, 0, 0)),
                  pl.BlockSpec((1, nl), lambda b, j, t: (0, j)),
                  pl.BlockSpec((bb, ns), lambda b, j, t: (b, j)),
                  pl.BlockSpec((bb, ns), lambda b, j, t: (b, j))],
        out_specs=[pl.BlockSpec((lt, bb, nl), lambda b, j, t: (t, b, j)),
                   pl.BlockSpec((bb, ns), lambda b, j, t: (b, j)),
                   pl.BlockSpec((bb, ns), lambda b, j, t: (b, j))],
        out_shape=(jax.ShapeDtypeStruct((seq, batch, D_MODEL), F32),
                   jax.ShapeDtypeStruct((batch, S5_LANES), F32),
                   jax.ShapeDtypeStruct((batch, S5_LANES), F32)),
        scratch_shapes=[pltpu.VMEM((lt, bb, ns), F32), pltpu.VMEM((lt, bb, ns), F32)],
        compiler_params=_cp(("arbitrary", "arbitrary", "arbitrary")),
    )(z, bb_re, bb_im, ab_re, ab_im, ct_re, ct_im, d_skip, h0_re, h0_im)
    return y, hr, hi


def _lru_kernel(x_ref, g_ref, cbuf_ref, cw_ref, cb_ref, wa_ref, ba_ref, wx_ref, bx_ref, lam_ref, h0_ref,
                y_ref, hn_ref, tail_ref, prev_s, a_s, h_s):
    lt, bb, w = x_ref.shape
    taps = cw_ref.shape[0]

    @pl.when(pl.program_id(2) == 0)
    def _():
        hn_ref[...] = h0_ref[...]
        prev_s[...] = cbuf_ref[...]

    xp = jnp.concatenate([prev_s[...], x_ref[...]], axis=0)
    xc = cb_ref[...] + xp[0:lt] * cw_ref[0]
    for tap in range(1, taps):
        xc = xc + xp[tap:tap + lt] * cw_ref[tap]
    prev_s[...] = xp[lt:lt + taps - 1]
    tail_ref[...] = xp[lt:lt + taps - 1]

    xc2 = xc.reshape(lt * bb, w)
    xb = xc2.astype(BF16)
    r = _sigmoid(jnp.dot(xb, wa_ref[0], preferred_element_type=F32) + ba_ref[...])
    i = _sigmoid(jnp.dot(xb, wx_ref[0], preferred_element_type=F32) + bx_ref[...])
    log_a = (-LRU_C) * r * _softplus(-lam_ref[...])
    a = jnp.exp(log_a)
    th = jnp.tanh(log_a)
    neg_expm1 = -2.0 * th / (1.0 - th)
    b = jnp.sqrt(neg_expm1) * (i * xc2)
    a_s[...] = a.reshape(lt, bb, w)
    h_s[...] = b.reshape(lt, bb, w)

    def body(t, h):
        h = a_s[t] * h + h_s[t]
        h_s[t] = h
        return h

    hn_ref[...] = lax.fori_loop(0, lt, body, hn_ref[...], unroll=min(lt, 8))
    y_ref[...] = h_s[...] * _gelu(g_ref[...])


def _lru_mixer(z, conv_buf_t, conv_w, conv_b, w_a, b_a, w_x, b_x, lam, h0, tokens=1024):
    seq, batch, _ = z.shape
    lt, bb = _tiles(seq, batch, tokens)
    w = LRU_BLOCK
    xcol, gcol = COL_XLRU // w, COL_GLRU // w
    taps = CONV_WIDTH
    vec = lambda a: a.reshape(1, D_MODEL)
    vspec = pl.BlockSpec((1, w), lambda b, h, t: (0, h))
    y, hn, tail = pl.pallas_call(
        _lru_kernel,
        grid=(batch // bb, LRU_HEADS, seq // lt),
        in_specs=[pl.BlockSpec((lt, bb, w), lambda b, h, t: (t, b, xcol + h)),
                  pl.BlockSpec((lt, bb, w), lambda b, h, t: (t, b, gcol + h)),
                  pl.BlockSpec((taps - 1, bb, w), lambda b, h, t: (0, b, h)),
                  pl.BlockSpec((taps, 1, w), lambda b, h, t: (0, 0, h)),
                  vspec,
                  pl.BlockSpec((1, w, w), lambda b, h, t: (h, 0, 0)),
                  vspec,
                  pl.BlockSpec((1, w, w), lambda b, h, t: (h, 0, 0)),
                  vspec, vspec,
                  pl.BlockSpec((bb, w), lambda b, h, t: (b, h))],
        out_specs=[pl.BlockSpec((lt, bb, w), lambda b, h, t: (t, b, h)),
                   pl.BlockSpec((bb, w), lambda b, h, t: (b, h)),
                   pl.BlockSpec((taps - 1, bb, w), lambda b, h, t: (0, b, h))],
        out_shape=(jax.ShapeDtypeStruct((seq, batch, D_MODEL), F32),
                   jax.ShapeDtypeStruct((batch, D_MODEL), F32),
                   jax.ShapeDtypeStruct((taps - 1, batch, D_MODEL), F32)),
        scratch_shapes=[pltpu.VMEM((taps - 1, bb, w), F32), pltpu.VMEM((lt, bb, w), F32),
                        pltpu.VMEM((lt, bb, w), F32)],
        compiler_params=_cp(("arbitrary", "arbitrary", "arbitrary")),
    )(z, z, conv_buf_t, conv_w.reshape(taps, 1, D_MODEL), vec(conv_b), w_a.astype(BF16), vec(b_a),
      w_x.astype(BF16), vec(b_x), vec(lam), h0)
    return y, hn, tail


def _rw_pre_kernel(zr_ref, zk_ref, zv_ref, zl_ref, shift_ref, mu_ref, mul_ref, w0_ref, w2_ref, a0_ref,
                   a2_ref, g2_ref, kk_ref, ka_ref,
                   r_out, w_out, k_out, v_out, kk_out, a_out, g_out, prev_s, prevl_s):
    lt, bb, d = zr_ref.shape
    nl = zl_ref.shape[-1]

    @pl.when(pl.program_id(1) == 0)
    def _():
        prev_s[...] = shift_ref[:, :, 0:3 * d]
        prevl_s[...] = shift_ref[:, :, 3 * d:3 * d + nl]

    def shifted(z_ref, prev, mu):
        z = z_ref[...]
        zp = jnp.concatenate([prev, z[0:lt - 1]], axis=0) if lt > 1 else prev
        return z + (zp - z) * mu, z[lt - 1:lt]

    r, last_r = shifted(zr_ref, prev_s[:, :, 0:d], mu_ref[:, :, 0:d])
    k, last_k = shifted(zk_ref, prev_s[:, :, d:2 * d], mu_ref[:, :, d:2 * d])
    v, last_v = shifted(zv_ref, prev_s[:, :, 2 * d:3 * d], mu_ref[:, :, 2 * d:3 * d])
    lo, last_l = shifted(zl_ref, prevl_s[...], mul_ref[...])
    prev_s[:, :, 0:d] = last_r
    prev_s[:, :, d:2 * d] = last_k
    prev_s[:, :, 2 * d:3 * d] = last_v
    prevl_s[...] = last_l

    n = lt * bb
    lo2 = lo.reshape(n, nl)
    wa_in = lo2[:, 0:LANES]
    g_in = lo2[:, LANES:3 * LANES]
    wpre = w0_ref[...] + jnp.dot(jnp.tanh(wa_in).astype(BF16), w2_ref[...], preferred_element_type=F32)
    wlog = -_softplus(-wpre) - 0.5
    decay = jnp.exp(-jnp.exp(wlog))
    a = _sigmoid(a0_ref[...] + jnp.dot(wa_in.astype(BF16), a2_ref[...], preferred_element_type=F32))
    g = jnp.dot(_sigmoid(g_in).astype(BF16), g2_ref[...], preferred_element_type=F32)
    r2 = r.reshape(n, d)
    k2 = k.reshape(n, d)
    shp = (lt, bb, d)
    r_out[...] = r
    w_out[...] = decay.reshape(shp)
    k_out[...] = (k2 * (1.0 + (a - 1.0) * ka_ref[...])).reshape(shp)
    v_out[...] = v
    kk_out[...] = (k2 * kk_ref[...]).reshape(shp)
    a_out[...] = a.reshape(shp)
    g_out[...] = g.reshape(shp)
    del r2


def _rw_pre(z, shift_p, mu_p, w0, w2p, a0, a2p, g2p, k_k, k_a, tokens=256):
    seq, batch, _ = z.shape
    lt, bb = _tiles(seq, batch, tokens)
    d = D_MODEL
    nl = LORA_PAD
    vec = lambda a: a.reshape(1, d)
    full = lambda shape: pl.BlockSpec(shape, lambda b, t: (0,) * len(shape))
    act = pl.BlockSpec((lt, bb, d), lambda b, t: (t, b, 0))
    out_shape = tuple(jax.ShapeDtypeStruct((seq, batch, d), F32) for _ in range(7))
    mu_main = mu_p[:, :, 0:3 * d]
    mu_lora = mu_p[:, :, 3 * d:]
    return pl.pallas_call(
        _rw_pre_kernel,
        grid=(batch // bb, seq // lt),
        in_specs=[pl.BlockSpec((lt, bb, d), lambda b, t: (t, b, COL_R // d)),
                  pl.BlockSpec((lt, bb, d), lambda b, t: (t, b, COL_K // d)),
                  pl.BlockSpec((lt, bb, d), lambda b, t: (t, b, COL_V // d)),
                  pl.BlockSpec((lt, bb, nl), lambda b, t: (t, b, COL_LORA // nl)),
                  pl.BlockSpec((1, bb, 3 * d + nl), lambda b, t: (0, b, 0)),
                  full((1, 1, 3 * d)), full((1, 1, nl)),
                  full((1, d)), full((LANES, d)), full((1, d)), full((LANES, d)), full((2 * LANES, d)),
                  full((1, d)), full((1, d))],
        out_specs=[act] * 7,
        out_shape=out_shape,
        scratch_shapes=[pltpu.VMEM((1, bb, 3 * d), F32), pltpu.VMEM((1, bb, nl), F32)],
        compiler_params=_cp(("arbitrary", "arbitrary")),
    )(z, z, z, z, shift_p, mu_main, mu_lora, vec(w0), w2p, vec(a0), a2p, g2p, vec(k_k), vec(k_a))


def _rw_scan_kernel(r_ref, w_ref, k_ref, v_ref, kk_ref, a_ref, rk_ref, lnw_ref, lnb_ref, s0_ref,
                    o_ref, s_ref, nkk_s, b_s, y_s):
    lt, hd, nl = r_ref.shape

    @pl.when(pl.program_id(1) == 0)
    def _():
        s_ref[...] = s0_ref[...]

    kk = kk_ref[...]
    kkn = kk * lax.rsqrt(jnp.maximum(jnp.sum(kk * kk, axis=1, keepdims=True), 1e-24))
    nkk_s[...] = -kkn
    b_s[...] = kkn * a_ref[...]

    def step(t, carry):
        acc = [jnp.zeros((hd, nl), F32) for _ in range(4)]
        for j in range(hd):
            acc[j % 4] = acc[j % 4] + s_ref[j] * nkk_s[t, pl.ds(j, 1), :]
        sa = (acc[0] + acc[1]) + (acc[2] + acc[3])
        vt = v_ref[t]
        yac = [jnp.zeros((hd, nl), F32) for _ in range(4)]
        for j in range(hd):
            sj = (s_ref[j] * w_ref[t, pl.ds(j, 1), :] + sa * b_s[t, pl.ds(j, 1), :]
                  + vt * k_ref[t, pl.ds(j, 1), :])
            s_ref[j] = sj
            yac[j % 4] = yac[j % 4] + sj * r_ref[t, pl.ds(j, 1), :]
        y_s[t] = (yac[0] + yac[1]) + (yac[2] + yac[3])
        return carry

    lax.fori_loop(0, lt, step, 0)
    ys = y_s[...]
    mean = jnp.mean(ys, axis=1, keepdims=True)
    yc = ys - mean
    var = jnp.mean(yc * yc, axis=1, keepdims=True)
    yn = yc * lax.rsqrt(var + RW_LN_EPS)
    bonus = jnp.sum(r_ref[...] * k_ref[...] * rk_ref[...], axis=1, keepdims=True)
    o_ref[...] = yn * lnw_ref[...] + lnb_ref[...] + bonus * v_ref[...]


def _rw_scan(r, w, k, v, kk, a, rk_t, lnw_t, lnb_t, s0_t, lt=32):
    seq, hd, n = r.shape
    lt = min(lt, seq)
    while seq % lt:
        lt -= 1
    nl = LANES
    act = pl.BlockSpec((lt, hd, nl), lambda c, t: (t, 0, c))
    par = pl.BlockSpec((1, hd, nl), lambda c, t: (0, 0, c))
    st = pl.BlockSpec((hd, hd, nl), lambda c, t: (0, 0, c))
    return pl.pallas_call(
        _rw_scan_kernel,
        grid=(n // nl, seq // lt),
        in_specs=[act] * 6 + [par] * 3 + [st],
        out_specs=[act, st],
        out_shape=(jax.ShapeDtypeStruct((seq, hd, n), F32), jax.ShapeDtypeStruct((hd, hd, n), F32)),
        scratch_shapes=[pltpu.VMEM((lt, hd, nl), F32)] * 3,
        compiler_params=_cp(("arbitrary", "arbitrary")),
    )(r, w, k, v, kk, a, rk_t, lnw_t, lnb_t, s0_t)


def _to_heads_t(x, batch):
    seq = x.shape[0]
    return x.reshape(seq, batch, RW_HEADS, RW_HEAD).transpose(0, 3, 1, 2).reshape(seq, RW_HEAD, batch * RW_HEADS)


def _from_heads_t(x, batch):
    seq = x.shape[0]
    return x.reshape(seq, RW_HEAD, batch, RW_HEADS).transpose(0, 2, 3, 1).reshape(seq, batch, D_MODEL)


def _head_param_t(p, batch):
    t = p.reshape(RW_HEADS, RW_HEAD).T
    return jnp.tile(t[:, None, :], (1, batch, 1)).reshape(1, RW_HEAD, batch * RW_HEADS)


def _merge_kernel(x_ref, ys5_ref, ylru_ref, orw_ref, grw_ref, zg1_ref, zg2_ref, zg3_ref,
                  gt1_ref, sc2_ref, sh2_ref, wglu_ref, bglu_ref, wb1_ref, wb2_ref, wb3_ref, wout_ref,
                  n2g_ref, wr_ref, br_ref,
                  x1_ref, h2_ref, lg_ref):
    lt, bb, d = x_ref.shape
    n = lt * bb
    mm = lambda a, w_ref: jnp.dot(a.astype(BF16), w_ref[...], preferred_element_type=F32)
    flat = lambda ref: ref[...].reshape(n, d)

    y1 = _gelu(flat(ys5_ref))
    y1 = y1 * _sigmoid(mm(y1, wglu_ref) + bglu_ref[...])
    y3 = flat(orw_ref) * flat(grw_ref)
    merged = (_sigmoid(flat(zg1_ref)) * mm(y1, wb1_ref)
              + _sigmoid(flat(zg2_ref)) * mm(flat(ylru_ref), wb2_ref)
              + _sigmoid(flat(zg3_ref)) * mm(y3, wb3_ref))
    upd = mm(merged, wout_ref).reshape(lt, bb, d)
    x1 = x_ref[...] + gt1_ref[...] * upd
    x1_ref[...] = x1
    y = x1 * lax.rsqrt(jnp.mean(x1 * x1, axis=-1, keepdims=True) + RMS_EPS) * n2g_ref[...]
    h2 = y * (1.0 + sc2_ref[...]) + sh2_ref[...]
    h2_ref[...] = h2.astype(BF16)
    logits = jnp.dot(h2.reshape(n, d), wr_ref[...], precision=HIGHEST, preferred_element_type=F32)
    lg_ref[...] = (logits + br_ref[...]).reshape(lt, bb, lg_ref.shape[-1])


def _merge(x, y_s5, y_lru, o_rw, g_rw, z, mod, w_glu, b_glu, wb1, wb2, wb3, w_out, norm2_g, wr_p, br_p,
           tokens=256):
    seq, batch, d = x.shape
    lt, bb = _tiles(seq, batch, tokens)
    act = pl.BlockSpec((lt, bb, d), lambda b, t: (t, b, 0))
    gate = lambda i: pl.BlockSpec((lt, bb, d), lambda b, t: (t, b, COL_GATE // d + i))
    modspec = lambda i: pl.BlockSpec((1, bb, d), lambda b, t: (0, b, i))
    wspec = pl.BlockSpec((d, d), lambda b, t: (0, 0), pipeline_mode=pl.Buffered(1))
    vspec = pl.BlockSpec((1, d), lambda b, t: (0, 0))
    ne = wr_p.shape[1]
    return pl.pallas_call(
        _merge_kernel,
        grid=(batch // bb, seq // lt),
        in_specs=[act] * 5 + [gate(0), gate(1), gate(2), modspec(2), modspec(4), modspec(3),
                              wspec, vspec, wspec, wspec, wspec, wspec,
                              pl.BlockSpec((1, 1, d), lambda b, t: (0, 0, 0)),
                              pl.BlockSpec((d, ne), lambda b, t: (0, 0)),
                              pl.BlockSpec((1, ne), lambda b, t: (0, 0))],
        out_specs=[act, act, pl.BlockSpec((lt, bb, ne), lambda b, t: (t, b, 0))],
        out_shape=(jax.ShapeDtypeStruct((seq, batch, d), F32),
                   jax.ShapeDtypeStruct((seq, batch, d), BF16),
                   jax.ShapeDtypeStruct((seq, batch, ne), F32)),
        compiler_params=_cp(("arbitrary", "arbitrary")),
    )(x, y_s5, y_lru, o_rw, g_rw, z, z, z, mod, mod, mod, w_glu, b_glu.reshape(1, d), wb1, wb2, wb3, w_out,
      norm2_g.reshape(1, 1, d), wr_p, br_p)


ROUTE_IDX, ROUTE_RANK, ROUTE_WT = 0, TOP_K, 2 * TOP_K


def _route_kernel(lg_ref, rec_ref, cnt_ref, run_s):
    tt, nl = lg_ref.shape

    @pl.when(pl.program_id(0) == 0)
    def _():
        run_s[...] = jnp.zeros_like(run_s)

    lane = lax.broadcasted_iota(jnp.int32, (tt, nl), 1).astype(F32)
    neg = jnp.float32(-jnp.inf)
    vals = jnp.where(lane < N_EXPERTS, lg_ref[...], neg)
    tops, hots = [], []
    for _ in range(TOP_K):
        m = jnp.max(vals, axis=-1, keepdims=True)
        idx = jnp.min(jnp.where(vals == m, lane, float(nl)), axis=-1, keepdims=True)
        hot = lane == idx
        vals = jnp.where(hot, neg, vals)
        tops.append((m, idx))
        hots.append(hot)
    es = [jnp.exp(m - tops[0][0]) for m, _ in tops]
    den = es[0]
    for e in es[1:]:
        den = den + e

    mask = jnp.zeros((tt, nl), F32)
    for hot in hots:
        mask = jnp.where(hot, 1.0, mask)
    row = lax.broadcasted_iota(jnp.int32, (tt, tt), 0)
    col = lax.broadcasted_iota(jnp.int32, (tt, tt), 1)
    tri = jnp.where(col < row, 1.0, 0.0).astype(BF16)
    prefix = jnp.dot(tri, mask.astype(BF16), preferred_element_type=F32) + run_s[...]
    run_s[...] = run_s[...] + jnp.sum(mask, axis=0, keepdims=True)
    cnt_ref[...] = run_s[...]

    rec = jnp.zeros((tt, nl), F32)
    for k in range(TOP_K):
        rank = jnp.sum(jnp.where(hots[k], prefix, 0.0), axis=-1, keepdims=True)
        rec = jnp.where(lane == ROUTE_IDX + k, tops[k][1].astype(F32), rec)
        rec = jnp.where(lane == ROUTE_RANK + k, rank, rec)
        rec = jnp.where(lane == ROUTE_WT + k, es[k] / den, rec)
    rec_ref[...] = rec


def _route(logits):
    t, nl = logits.shape
    tt = min(ROUTE_TILE, t)
    while t % tt:
        tt -= SUBLANES
    return pl.pallas_call(
        _route_kernel,
        grid=(t // tt,),
        in_specs=[pl.BlockSpec((tt, nl), lambda i: (i, 0))],
        out_specs=[pl.BlockSpec((tt, nl), lambda i: (i, 0)), pl.BlockSpec((1, nl), lambda i: (0, 0))],
        out_shape=(jax.ShapeDtypeStruct((t, nl), F32), jax.ShapeDtypeStruct((1, nl), F32)),
        scratch_shapes=[pltpu.VMEM((1, nl), F32)],
        compiler_params=_cp(("arbitrary",)),
    )(logits)


def _expert_kernel(te_ref, tv_ref, xs_ref, wgu_ref, bgu_ref, wd_ref, bd_ref, o_ref, wgu_s, wd_s):
    i = pl.program_id(0)
    de = wd_s.shape[0]
    changed = jnp.logical_or(i == 0, te_ref[i] != te_ref[jnp.maximum(i - 1, 0)])

    @pl.when(changed)
    def _():
        wgu_s[...] = wgu_ref[0].astype(BF16)
        wd_s[...] = wd_ref[0].astype(BF16)

    @pl.when(tv_ref[i] > 0)
    def _():
        gu = jnp.dot(xs_ref[...], wgu_s[...], preferred_element_type=F32) + bgu_ref[0]
        glu = jnp.minimum(gu[:, :de], SWIGLU_LIMIT)
        lin = jnp.clip(gu[:, de:], -SWIGLU_LIMIT, SWIGLU_LIMIT)
        act = glu * _sigmoid(SWIGLU_ALPHA * glu) * (lin + 1.0)
        o_ref[...] = jnp.dot(act.astype(BF16), wd_s[...], preferred_element_type=F32) + bd_ref[0]

    @pl.when(tv_ref[i] == 0)
    def _():
        o_ref[...] = jnp.zeros_like(o_ref)


def _experts(tile_expert, tile_valid, xs, w_gu, b_gu, w_down, b_down):
    rows, d = xs.shape
    ne, _, n2 = w_gu.shape
    de = w_down.shape[1]
    tm = EXPERT_TILE
    grid_spec = pltpu.PrefetchScalarGridSpec(
        num_scalar_prefetch=2,
        grid=(rows // tm,),
        in_specs=[pl.BlockSpec((tm, d), lambda i, te, tv: (i, 0)),
                  pl.BlockSpec((1, d, n2), lambda i, te, tv: (te[i], 0, 0)),
                  pl.BlockSpec((1, 1, n2), lambda i, te, tv: (te[i], 0, 0)),
                  pl.BlockSpec((1, de, d), lambda i, te, tv: (te[i], 0, 0)),
                  pl.BlockSpec((1, 1, d), lambda i, te, tv: (te[i], 0, 0))],
        out_specs=pl.BlockSpec((tm, d), lambda i, te, tv: (i, 0)),
        scratch_shapes=[pltpu.VMEM((d, n2), BF16), pltpu.VMEM((de, d), BF16)])
    return pl.pallas_call(
        _expert_kernel,
        grid_spec=grid_spec,
        out_shape=jax.ShapeDtypeStruct((rows, d), F32),
        compiler_params=_cp(("arbitrary",)),
    )(tile_expert, tile_valid, xs, w_gu, b_gu.reshape(ne, 1, n2), w_down, b_down.reshape(ne, 1, d))


def _combine_kernel(x_ref, og_ref, rec_ref, gt_ref, fg_ref, x2_ref, *, final):
    d = x_ref.shape[-1]
    rec = rec_ref[...]
    y = rec[:, :, ROUTE_WT:ROUTE_WT + 1] * og_ref[:, :, 0:d]
    for k in range(1, TOP_K):
        y = y + rec[:, :, ROUTE_WT + k:ROUTE_WT + k + 1] * og_ref[:, :, k * d:(k + 1) * d]
    x2 = x_ref[...] + gt_ref[...] * y
    if final:
        x2 = x2 * lax.rsqrt(jnp.mean(x2 * x2, axis=-1, keepdims=True) + RMS_EPS) * fg_ref[...]
    x2_ref[...] = x2


def _combine(x1, og, rec, mod, final_g, final, tokens=256):
    seq, batch, d = x1.shape
    lt, bb = _tiles(seq, batch, tokens)
    nl = rec.shape[-1]
    return pl.pallas_call(
        functools.partial(_combine_kernel, final=final),
        grid=(batch // bb, seq // lt),
        in_specs=[pl.BlockSpec((lt, bb, d), lambda b, t: (t, b, 0)),
                  pl.BlockSpec((lt, bb, TOP_K * d), lambda b, t: (t, b, 0)),
                  pl.BlockSpec((lt, bb, nl), lambda b, t: (t, b, 0)),
                  pl.BlockSpec((1, bb, d), lambda b, t: (0, b, 5)),
                  pl.BlockSpec((1, 1, d), lambda b, t: (0, 0, 0))],
        out_specs=pl.BlockSpec((lt, bb, d), lambda b, t: (t, b, 0)),
        out_shape=jax.ShapeDtypeStruct((seq, batch, d), F32),
        compiler_params=_cp(("arbitrary", "arbitrary")),
    )(x1, og, rec, mod, final_g.reshape(1, 1, d))


def _pad_in_cols(a, axis):
    d = D_MODEL
    main = lax.slice_in_dim(a, 0, 6 * d, axis=axis)
    lora = lax.slice_in_dim(a, 6 * d, 6 * d + RW_LORA, axis=axis)
    gates = lax.slice_in_dim(a, 6 * d + RW_LORA, 9 * d + RW_LORA, axis=axis)
    pad_shape = list(a.shape)
    pad_shape[axis] = LORA_PAD - RW_LORA
    return jnp.concatenate([main, gates, lora, jnp.zeros(pad_shape, a.dtype)], axis=axis)


def _pad_rw_cols(a):
    pad = jnp.zeros(a.shape[:-1] + (LORA_PAD - RW_LORA,), a.dtype)
    return jnp.concatenate([a, pad], axis=-1)


def _mix_group(x, mod, st, lp, s5p):
    seq, batch, d = x.shape
    s5_re, s5_im, lru_h, lru_conv, rw_s, rw_shift = st
    z = _in_proj(x, mod, lp["norm1_g"], lp["w_in_p"])

    y_s5, n_s5_re, n_s5_im = _s5_mixer(
        z, s5p["ab_re"], s5p["ab_im"], s5p["bb_re"], s5p["bb_im"], s5p["ct_re"], s5p["ct_im"], s5p["d"],
        s5_re.reshape(batch, S5_LANES), s5_im.reshape(batch, S5_LANES))

    y_lru, n_lru_h, tail = _lru_mixer(
        z, lru_conv.transpose(1, 0, 2), lp["lru_conv_w"], lp["lru_conv_b"], lp["lru_w_a"], lp["lru_b_a"],
        lp["lru_w_x"], lp["lru_b_x"], lp["lru_lam"], lru_h)

    shift_p = _pad_rw_cols(rw_shift)[None]
    r, w, k, v, kk, a, g = _rw_pre(z, shift_p, lp["rw_mu_p"], lp["rw_w0"], lp["rw_w2p"], lp["rw_a0"],
                                   lp["rw_a2p"], lp["rw_g2p"], lp["rw_k_k"], lp["rw_k_a"])
    n = batch * RW_HEADS
    s0_t = rw_s.reshape(n, RW_HEAD, RW_HEAD).transpose(2, 1, 0)
    o_t, sn_t = _rw_scan(*(_to_heads_t(t, batch) for t in (r, w, k, v, kk, a)),
                         _head_param_t(lp["rw_r_k"].reshape(-1), batch),
                         _head_param_t(lp["rw_ln_w"], batch), _head_param_t(lp["rw_ln_b"], batch), s0_t)
    o_rw = _from_heads_t(o_t, batch)
    n_rw_s = sn_t.transpose(2, 1, 0).reshape(batch, RW_HEADS, RW_HEAD, RW_HEAD)
    n_rw_shift = jnp.concatenate([z[seq - 1, :, COL_R:COL_R + 3 * d], z[seq - 1, :, COL_LORA:COL_LORA + RW_LORA]],
                                 axis=-1)

    x1, h2, logits = _merge(x, y_s5, y_lru, o_rw, g, z, mod, lp["s5_w_glu"], lp["s5_b_glu"], lp["w_br_s5"],
                            lp["w_br_lru"], lp["w_br_rw"], lp["w_out"], lp["norm2_g"], lp["wr_p"], lp["br_p"])
    new = (n_s5_re.reshape(batch, S5_GROUPS, S5_STATE), n_s5_im.reshape(batch, S5_GROUPS, S5_STATE),
           n_lru_h, tail.transpose(1, 0, 2), n_rw_s, n_rw_shift)
    return x1, h2, logits, new


def _moe(h2_all, logits_all, lp):
    t, d = h2_all.shape
    tm = EXPERT_TILE
    rec, cnt = _route(logits_all)
    idx = rec[:, ROUTE_IDX:ROUTE_IDX + TOP_K].astype(jnp.int32)
    rank = rec[:, ROUTE_RANK:ROUTE_RANK + TOP_K].astype(jnp.int32)
    counts = cnt[0, :N_EXPERTS].astype(jnp.int32)
    padded = ((counts + tm - 1) // tm) * tm
    ends = jnp.cumsum(padded)
    offs = ends - padded
    pos = offs[idx] + rank
    n_tiles = (t * TOP_K + N_EXPERTS * (tm - 1) + tm - 1) // tm
    rows = n_tiles * tm
    src = jnp.zeros((rows,), jnp.int32).at[pos.reshape(-1)].set(
        jnp.repeat(jnp.arange(t, dtype=jnp.int32), TOP_K))
    starts = jnp.arange(n_tiles, dtype=jnp.int32) * tm
    tile_valid = (starts < ends[-1]).astype(jnp.int32)
    last = jnp.searchsorted(ends, ends[-1] - 1, side="right").astype(jnp.int32)
    tile_expert = jnp.where(tile_valid > 0, jnp.searchsorted(ends, starts, side="right").astype(jnp.int32), last)
    tile_expert = jnp.clip(tile_expert, 0, N_EXPERTS - 1)
    xs = jnp.take(h2_all, src, axis=0)
    os_ = _experts(tile_expert, tile_valid, xs, lp["moe_w_gu"], lp["moe_b_gu"], lp["moe_w_down"], lp["moe_b_down"])
    og = jnp.take(os_, pos.reshape(-1), axis=0).reshape(t, TOP_K * d)
    return og, rec


def kernel(x_prompt, x_sample, state_s5_re, state_s5_im, state_lru_h, cache_lru_conv, state_rwkv, cache_rwkv_shift, c_prompt, c_sample, w_mod, b_mod, norm1_g, w_in, s5_lam_re, s5_lam_im, s5_log_step, s5_b_re, s5_b_im, s5_c_re, s5_c_im, s5_d, s5_w_glu, s5_b_glu, lru_conv_w, lru_conv_b, lru_w_a, lru_b_a, lru_w_x, lru_b_x, lru_lam, rw_mu, rw_w0, rw_w2, rw_a0, rw_a2, rw_g2, rw_k_k, rw_k_a, rw_r_k, rw_ln_w, rw_ln_b, w_br_s5, w_br_lru, w_br_rw, w_out, norm2_g, moe_w_router, moe_b_router, moe_w_gu, moe_b_gu, moe_w_down, moe_b_down, final_g):
    depth = w_mod.shape[0]
    d = D_MODEL
    bp, lp_len = x_prompt.shape[0], x_prompt.shape[1]
    bs, ls_len = x_sample.shape[0], x_sample.shape[1]

    mod_all = _modulation(jnp.concatenate([c_prompt, c_sample], axis=0), w_mod, b_mod)

    ab_re, ab_im, q_re, q_im = _s5_params(s5_lam_re, s5_lam_im, s5_log_step)
    shp = (depth, S5_GROUPS, S5_STATE)
    q_re, q_im = q_re.reshape(shp)[..., None], q_im.reshape(shp)[..., None]
    bbar_re = q_re * s5_b_re - q_im * s5_b_im
    bbar_im = q_re * s5_b_im + q_im * s5_b_re
    per = LANES // S5_GROUP

    xs = [x_prompt.transpose(1, 0, 2), x_sample.transpose(1, 0, 2)]
    zeros_like_state = lambda s, b: jnp.zeros((b,) + s.shape[2:], F32)
    sample_states = (state_s5_re, state_s5_im, state_lru_h, cache_lru_conv, state_rwkv, cache_rwkv_shift)
    collected = [tuple([] for _ in sample_states), tuple([] for _ in sample_states)]

    for l in range(depth):
        lp = dict(
            norm1_g=norm1_g[l], w_in_p=_pad_in_cols(w_in[l], 1).astype(BF16),
            lru_conv_w=lru_conv_w[l], lru_conv_b=lru_conv_b[l], lru_w_a=lru_w_a[l], lru_b_a=lru_b_a[l],
            lru_w_x=lru_w_x[l], lru_b_x=lru_b_x[l], lru_lam=lru_lam[l],
            rw_mu_p=_pad_rw_cols(rw_mu[l]).reshape(1, 1, -1), rw_w0=rw_w0[l], rw_a0=rw_a0[l],
            rw_w2p=jnp.concatenate([rw_w2[l], jnp.zeros((LANES - RW_W_LORA, d), F32)], axis=0).astype(BF16),
            rw_a2p=jnp.concatenate([jnp.zeros((RW_W_LORA, d), F32), rw_a2[l]], axis=0).astype(BF16),
            rw_g2p=jnp.concatenate([rw_g2[l], jnp.zeros((2 * LANES - RW_G_LORA, d), F32)], axis=0).astype(BF16),
            rw_k_k=rw_k_k[l], rw_k_a=rw_k_a[l], rw_r_k=rw_r_k[l], rw_ln_w=rw_ln_w[l], rw_ln_b=rw_ln_b[l],
            s5_w_glu=s5_w_glu[l].astype(BF16), s5_b_glu=s5_b_glu[l], w_br_s5=w_br_s5[l].astype(BF16),
            w_br_lru=w_br_lru[l].astype(BF16), w_br_rw=w_br_rw[l].astype(BF16), w_out=w_out[l].astype(BF16),
            norm2_g=norm2_g[l],
            wr_p=jnp.concatenate([moe_w_router[l], jnp.zeros((d, LANES - N_EXPERTS), F32)], axis=1),
            br_p=jnp.concatenate([moe_b_router[l], jnp.zeros((LANES - N_EXPERTS,), F32)]).reshape(1, LANES),
            moe_w_gu=moe_w_gu[l], moe_b_gu=moe_b_gu[l], moe_w_down=moe_w_down[l], moe_b_down=moe_b_down[l])
        g0 = l * S5_GROUPS
        s5p = dict(
            ab_re=ab_re[g0:g0 + S5_GROUPS].reshape(1, S5_LANES), ab_im=ab_im[g0:g0 + S5_GROUPS].reshape(1, S5_LANES),
            bb_re=_block_diag(bbar_re[l].transpose(0, 2, 1), per).astype(BF16),
            bb_im=_block_diag(bbar_im[l].transpose(0, 2, 1), per).astype(BF16),
            ct_re=_block_diag(s5_c_re[l].transpose(0, 2, 1), per).astype(BF16),
            ct_im=_block_diag(s5_c_im[l].transpose(0, 2, 1), per).astype(BF16),
            d=s5_d[l].reshape(1, d))

        mods = [mod_all[l, :bp][None], mod_all[l, bp:][None]]
        states = [tuple(zeros_like_state(s, bp) for s in sample_states), tuple(s[l] for s in sample_states)]
        x1s, h2s, lgs = [], [], []
        for gi in range(2):
            x1, h2, lg, new = _mix_group(xs[gi], mods[gi], states[gi], lp, s5p)
            x1s.append(x1)
            h2s.append(h2.reshape(-1, d))
            lgs.append(lg.reshape(-1, LANES))
            for lst, s in zip(collected[gi], new):
                lst.append(s)

        og, rec = _moe(jnp.concatenate(h2s, axis=0), jnp.concatenate(lgs, axis=0), lp)
        tp = lp_len * bp
        ogs = [og[:tp].reshape(lp_len, bp, TOP_K * d), og[tp:].reshape(ls_len, bs, TOP_K * d)]
        recs = [rec[:tp].reshape(lp_len, bp, LANES), rec[tp:].reshape(ls_len, bs, LANES)]
        xs = [_combine(x1s[gi], ogs[gi], recs[gi], mods[gi], final_g, final=(l == depth - 1)) for gi in range(2)]

    y_prompt = xs[0].transpose(1, 0, 2)
    y_sample = xs[1].transpose(1, 0, 2)
    p_states = tuple(jnp.stack(lst) for lst in collected[0])
    s_states = tuple(jnp.stack(lst) for lst in collected[1])
    return (y_prompt, y_sample) + p_states + s_states
```

```python
import functools
import math

import jax
import jax.numpy as jnp
from jax import lax
from jax.experimental import pallas as pl
from jax.experimental.pallas import tpu as pltpu

F32 = jnp.float32
BF16 = jnp.bfloat16
HIGHEST = lax.Precision.HIGHEST

D_MODEL = 1024
RMS_EPS = 1e-5
S5_GROUP = 16
S5_GROUPS = D_MODEL // S5_GROUP
S5_STATE = 64
S5_LANES = S5_GROUPS * S5_STATE
LRU_HEADS = 8
LRU_BLOCK = D_MODEL // LRU_HEADS
CONV_WIDTH = 4
LRU_C = 8.0
RW_HEAD = 64
RW_HEADS = D_MODEL // RW_HEAD
RW_W_LORA = 64
RW_A_LORA = 64
RW_G_LORA = 160
RW_LORA = RW_W_LORA + RW_A_LORA + RW_G_LORA
RW_COLS = 3 * D_MODEL + RW_LORA
RW_LN_EPS = 64e-5
N_EXPERTS = 32
TOP_K = 4
SWIGLU_ALPHA = 1.702
SWIGLU_LIMIT = 7.0

LANES = 128
SUBLANES = 8
VMEM_LIMIT = 56 * 1024 * 1024

LORA_PAD = 512
COL_S5 = 0
COL_XLRU = 1 * D_MODEL
COL_GLRU = 2 * D_MODEL
COL_R = 3 * D_MODEL
COL_K = 4 * D_MODEL
COL_V = 5 * D_MODEL
COL_GATE = 6 * D_MODEL
COL_LORA = 9 * D_MODEL
D_IN_PAD = COL_LORA + LORA_PAD

S5_LANE_BLOCK = LANES
S5_STATE_BLOCK = (LANES // S5_GROUP) * S5_STATE
EXPERT_TILE = 512
ROUTE_TILE = 512


def _cp(sem, vmem=VMEM_LIMIT):
    return pltpu.CompilerParams(dimension_semantics=sem, vmem_limit_bytes=vmem)


def _gelu(x):
    return 0.5 * x * (1.0 + jnp.tanh(math.sqrt(2.0 / math.pi) * (x + 0.044715 * (x * x * x))))


def _sigmoid(x):
    return 1.0 / (1.0 + jnp.exp(-x))


def _softplus(x):
    return jnp.maximum(x, 0.0) + jnp.log1p(jnp.exp(-jnp.abs(x)))


def _tiles(seq, batch, tokens):
    bb = min(batch, max(SUBLANES, (tokens // seq) // SUBLANES * SUBLANES))
    while batch % bb:
        bb -= SUBLANES
    lt = max(1, min(seq, tokens // bb))
    while seq % lt:
        lt -= 1
    return lt, bb


def _mod_kernel(c_ref, w_ref, b_ref, o_ref):
    c = c_ref[...]
    s = c * _sigmoid(c)
    o_ref[0] = jnp.dot(s, w_ref[0], precision=HIGHEST, preferred_element_type=F32) + b_ref[0]


def _modulation(c, w_mod, b_mod):
    depth, d, n = w_mod.shape
    bc = c.shape[0]
    tn = 1536
    return pl.pallas_call(
        _mod_kernel,
        grid=(depth, n // tn),
        in_specs=[pl.BlockSpec((bc, d), lambda l, j: (0, 0)),
                  pl.BlockSpec((1, d, tn), lambda l, j: (l, 0, j)),
                  pl.BlockSpec((1, 1, tn), lambda l, j: (l, 0, j))],
        out_specs=pl.BlockSpec((1, bc, tn), lambda l, j: (l, 0, j)),
        out_shape=jax.ShapeDtypeStruct((depth, bc, n), F32),
        compiler_params=_cp(("arbitrary", "arbitrary")),
    )(c, w_mod, b_mod.reshape(depth, 1, n))


def _in_kernel(x_ref, sh_ref, sc_ref, g_ref, w_ref, o_ref, h_scr):
    lt, bb, d = x_ref.shape

    @pl.when(pl.program_id(2) == 0)
    def _():
        x = x_ref[...]
        y = x * lax.rsqrt(jnp.mean(x * x, axis=-1, keepdims=True) + RMS_EPS) * g_ref[...]
        h = y * (1.0 + sc_ref[...]) + sh_ref[...]
        h_scr[...] = h.reshape(lt * bb, d).astype(BF16)

    o = jnp.dot(h_scr[...], w_ref[...], preferred_element_type=F32)
    o_ref[...] = o.reshape(lt, bb, o.shape[-1])


def _in_proj(x, mod, norm_g, w_in_p, tokens=1024, tn=512):
    seq, batch, d = x.shape
    lt, bb = _tiles(seq, batch, tokens)
    n = w_in_p.shape[1]
    return pl.pallas_call(
        _in_kernel,
        grid=(batch // bb, seq // lt, n // tn),
        in_specs=[pl.BlockSpec((lt, bb, d), lambda b, t, j: (t, b, 0)),
                  pl.BlockSpec((1, bb, d), lambda b, t, j: (0, b, 0)),
                  pl.BlockSpec((1, bb, d), lambda b, t, j: (0, b, 1)),
                  pl.BlockSpec((1, 1, d), lambda b, t, j: (0, 0, 0)),
                  pl.BlockSpec((d, tn), lambda b, t, j: (0, j))],
        out_specs=pl.BlockSpec((lt, bb, tn), lambda b, t, j: (t, b, j)),
        out_shape=jax.ShapeDtypeStruct((seq, batch, n), F32),
        scratch_shapes=[pltpu.VMEM((lt * bb, d), BF16)],
        compiler_params=_cp(("arbitrary", "arbitrary", "arbitrary")),
    )(x, mod, mod, norm_g.reshape(1, 1, d), w_in_p)


def _s5_kernel(u_ref, bbr_ref, bbi_ref, ar_ref, ai_ref, ctr_ref, cti_ref, d_ref, h0r_ref, h0i_ref,
               y_ref, hr_ref, hi_ref, sr, si):
    lt, bb, nl = u_ref.shape
    ns = sr.shape[-1]

    @pl.when(pl.program_id(2) == 0)
    def _():
        hr_ref[...] = h0r_ref[...]
        hi_ref[...] = h0i_ref[...]

    u2 = u_ref[...].reshape(lt * bb, nl)
    ub = u2.astype(BF16)
    sr[...] = jnp.dot(ub, bbr_ref[0], preferred_element_type=F32).reshape(lt, bb, ns)
    si[...] = jnp.dot(ub, bbi_ref[0], preferred_element_type=F32).reshape(lt, bb, ns)
    ar = jnp.broadcast_to(ar_ref[...], (bb, ns))
    ai = jnp.broadcast_to(ai_ref[...], (bb, ns))

    def body(i, carry):
        hr, hi = carry
        nr = ar * hr - ai * hi + sr[i]
        ni = ar * hi + ai * hr + si[i]
        sr[i] = nr
        si[i] = ni
        return nr, ni

    hr, hi = lax.fori_loop(0, lt, body, (hr_ref[...], hi_ref[...]), unroll=min(lt, 8))
    hr_ref[...] = hr
    hi_ref[...] = hi
    y = (jnp.dot(sr[...].reshape(lt * bb, ns).astype(BF16), ctr_ref[0], preferred_element_type=F32)
         - jnp.dot(si[...].reshape(lt * bb, ns).astype(BF16), cti_ref[0], preferred_element_type=F32)
         + d_ref[...] * u2)
    y_ref[...] = y.reshape(lt, bb, nl)


def _s5_params_kernel(lr_ref, li_ref, ls_ref, abr_ref, abi_ref, qr_ref, qi_ref):
    lr = lr_ref[...]
    li = li_ref[...]
    step = jnp.exp(ls_ref[...])
    mag = jnp.exp(lr * step)
    ab_re = mag * jnp.cos(li * step)
    ab_im = mag * jnp.sin(li * step)
    den = lr * lr + li * li
    abr_ref[...] = ab_re
    abi_ref[...] = ab_im
    qr_ref[...] = ((ab_re - 1.0) * lr + ab_im * li) / den
    qi_ref[...] = (ab_im * lr - (ab_re - 1.0) * li) / den


def _s5_params(lam_re, lam_im, log_step):
    depth, g, n = lam_re.shape
    shp = jax.ShapeDtypeStruct((depth * g, n), F32)
    return pl.pallas_call(_s5_params_kernel, out_shape=(shp, shp, shp, shp))(
        lam_re.reshape(depth * g, n), lam_im.reshape(depth * g, n), log_step.reshape(depth * g, 1))


def _block_diag(blocks, per):
    g, a, b = blocks.shape
    x = blocks.reshape(g // per, per, a, b)
    eye = jnp.eye(per, dtype=blocks.dtype)
    return jnp.einsum("gpab,pq->gpaqb", x, eye).reshape(g // per, per * a, per * b)


def _s5_mixer(z, ab_re, ab_im, bb_re, bb_im, ct_re, ct_im, d_skip, h0_re, h0_im, tokens=512):
    seq, batch, _ = z.shape
    lt, bb = _tiles(seq, batch, tokens)
    nl, ns = S5_LANE_BLOCK, S5_STATE_BLOCK
    nblk = D_MODEL // nl
    col0 = COL_S5 // nl
    y, hr, hi = pl.pallas_call(
        _s5_kernel,
        grid=(batch // bb, nblk, seq // lt),
        in_specs=[pl.BlockSpec((lt, bb, nl), lambda b, j, t: (t, b, col0 + j)),
                  pl.BlockSpec((1, nl, ns), lambda b, j, t: (j, 0, 0)),
                  pl.BlockSpec((1, nl, ns), lambda b, j, t: (j, 0, 0)),
                  pl.BlockSpec((1, ns), lambda b, j, t: (0, j)),
                  pl.BlockSpec((1, ns), lambda b, j, t: (0, j)),
                  pl.BlockSpec((1, ns, nl), lambda b, j, t: (j, 0, 0)),
                  pl.BlockSpec((1, ns, nl), lambda b, j, t: (j, 0, 0)),
                  pl.BlockSpec((1, nl), lambda b, j, t: (0, j)),
                  pl.BlockSpec((bb, ns), lambda b, j, t: (b, j)),
                  pl.BlockSpec((bb, ns), lambda b, j, t: (b, j))],
        out_specs=[pl.BlockSpec((lt, bb, nl), lambda b, j, t: (t, b, j)),
                   pl.BlockSpec((bb, ns), lambda b, j, t: (b, j)),
                   pl.BlockSpec((bb, ns), lambda b, j, t: (b, j))],
        out_shape=(jax.ShapeDtypeStruct((seq, batch, D_MODEL), F32),
                   jax.ShapeDtypeStruct((batch, S5_LANES), F32),
                   jax.ShapeDtypeStruct((batch, S5_LANES), F32)),
        scratch_shapes=[pltpu.VMEM((lt, bb, ns), F32), pltpu.VMEM((lt, bb, ns), F32)],
        compiler_params=_cp(("arbitrary", "arbitrary", "arbitrary")),
    )(z, bb_re, bb_im, ab_re, ab_im, ct_re, ct_im, d_skip, h0_re, h0_im)
    return y, hr, hi


def _lru_kernel(x_ref, g_ref, cbuf_ref, cw_ref, cb_ref, wa_ref, ba_ref, wx_ref, bx_ref, lam_ref, h0_ref,
                y_ref, hn_ref, tail_ref, prev_s, a_s, h_s):
    lt, bb, w = x_ref.shape
    taps = cw_ref.shape[0]

    @pl.when(pl.program_id(2) == 0)
    def _():
        hn_ref[...] = h0_ref[...]
        prev_s[...] = cbuf_ref[...]

    xp = jnp.concatenate([prev_s[...], x_ref[...]], axis=0)
    xc = cb_ref[...] + xp[0:lt] * cw_ref[0]
    for tap in range(1, taps):
        xc = xc + xp[tap:tap + lt] * cw_ref[tap]
    prev_s[...] = xp[lt:lt + taps - 1]
    tail_ref[...] = xp[lt:lt + taps - 1]

    xc2 = xc.reshape(lt * bb, w)
    xb = xc2.astype(BF16)
    r = _sigmoid(jnp.dot(xb, wa_ref[0], preferred_element_type=F32) + ba_ref[...])
    i = _sigmoid(jnp.dot(xb, wx_ref[0], preferred_element_type=F32) + bx_ref[...])
    log_a = (-LRU_C) * r * _softplus(-lam_ref[...])
    a = jnp.exp(log_a)
    th = jnp.tanh(log_a)
    neg_expm1 = -2.0 * th / (1.0 - th)
    b = jnp.sqrt(neg_expm1) * (i * xc2)
    a_s[...] = a.reshape(lt, bb, w)
    h_s[...] = b.reshape(lt, bb, w)

    def body(t, h):
        h = a_s[t] * h + h_s[t]
        h_s[t] = h
        return h

    hn_ref[...] = lax.fori_loop(0, lt, body, hn_ref[...], unroll=min(lt, 8))
    y_ref[...] = h_s[...] * _gelu(g_ref[...])


def _lru_mixer(z, conv_buf_t, conv_w, conv_b, w_a, b_a, w_x, b_x, lam, h0, tokens=1024):
    seq, batch, _ = z.shape
    lt, bb = _tiles(seq, batch, tokens)
    w = LRU_BLOCK
    xcol, gcol = COL_XLRU // w, COL_GLRU // w
    taps = CONV_WIDTH
    vec = lambda a: a.reshape(1, D_MODEL)
    vspec = pl.BlockSpec((1, w), lambda b, h, t: (0, h))
    y, hn, tail = pl.pallas_call(
        _lru_kernel,
        grid=(batch // bb, LRU_HEADS, seq // lt),
        in_specs=[pl.BlockSpec((lt, bb, w), lambda b, h, t: (t, b, xcol + h)),
                  pl.BlockSpec((lt, bb, w), lambda b, h, t: (t, b, gcol + h)),
                  pl.BlockSpec((taps - 1, bb, w), lambda b, h, t: (0, b, h)),
                  pl.BlockSpec((taps, 1, w), lambda b, h, t: (0, 0, h)),
                  vspec,
                  pl.BlockSpec((1, w, w), lambda b, h, t: (h, 0, 0)),
                  vspec,
                  pl.BlockSpec((1, w, w), lambda b, h, t: (h, 0, 0)),
                  vspec, vspec,
                  pl.BlockSpec((bb, w), lambda b, h, t: (b, h))],
        out_specs=[pl.BlockSpec((lt, bb, w), lambda b, h, t: (t, b, h)),
                   pl.BlockSpec((bb, w), lambda b, h, t: (b, h)),
                   pl.BlockSpec((taps - 1, bb, w), lambda b, h, t: (0, b, h))],
        out_shape=(jax.ShapeDtypeStruct((seq, batch, D_MODEL), F32),
                   jax.ShapeDtypeStruct((batch, D_MODEL), F32),
                   jax.ShapeDtypeStruct((taps - 1, batch, D_MODEL), F32)),
        scratch_shapes=[pltpu.VMEM((taps - 1, bb, w), F32), pltpu.VMEM((lt, bb, w), F32),
                        pltpu.VMEM((lt, bb, w), F32)],
        compiler_params=_cp(("arbitrary", "arbitrary", "arbitrary")),
    )(z, z, conv_buf_t, conv_w.reshape(taps, 1, D_MODEL), vec(conv_b), w_a.astype(BF16), vec(b_a),
      w_x.astype(BF16), vec(b_x), vec(lam), h0)
    return y, hn, tail


def _rw_pre_kernel(zr_ref, zk_ref, zv_ref, zl_ref, shift_ref, mu_ref, mul_ref, w0_ref, w2_ref, a0_ref,
                   a2_ref, g2_ref, kk_ref, ka_ref,
                   r_out, w_out, k_out, v_out, kk_out, a_out, g_out, prev_s, prevl_s):
    lt, bb, d = zr_ref.shape
    nl = zl_ref.shape[-1]

    @pl.when(pl.program_id(1) == 0)
    def _():
        prev_s[...] = shift_ref[:, :, 0:3 * d]
        prevl_s[...] = shift_ref[:, :, 3 * d:3 * d + nl]

    def shifted(z_ref, prev, mu):
        z = z_ref[...]
        zp = jnp.concatenate([prev, z[0:lt - 1]], axis=0) if lt > 1 else prev
        return z + (zp - z) * mu, z[lt - 1:lt]

    r, last_r = shifted(zr_ref, prev_s[:, :, 0:d], mu_ref[:, :, 0:d])
    k, last_k = shifted(zk_ref, prev_s[:, :, d:2 * d], mu_ref[:, :, d:2 * d])
    v, last_v = shifted(zv_ref, prev_s[:, :, 2 * d:3 * d], mu_ref[:, :, 2 * d:3 * d])
    lo, last_l = shifted(zl_ref, prevl_s[...], mul_ref[...])
    prev_s[:, :, 0:d] = last_r
    prev_s[:, :, d:2 * d] = last_k
    prev_s[:, :, 2 * d:3 * d] = last_v
    prevl_s[...] = last_l

    n = lt * bb
    lo2 = lo.reshape(n, nl)
    wa_in = lo2[:, 0:LANES]
    g_in = lo2[:, LANES:3 * LANES]
    wpre = w0_ref[...] + jnp.dot(jnp.tanh(wa_in).astype(BF16), w2_ref[...], preferred_element_type=F32)
    wlog = -_softplus(-wpre) - 0.5
    decay = jnp.exp(-jnp.exp(wlog))
    a = _sigmoid(a0_ref[...] + jnp.dot(wa_in.astype(BF16), a2_ref[...], preferred_element_type=F32))
    g = jnp.dot(_sigmoid(g_in).astype(BF16), g2_ref[...], preferred_element_type=F32)
    r2 = r.reshape(n, d)
    k2 = k.reshape(n, d)
    shp = (lt, bb, d)
    r_out[...] = r
    w_out[...] = decay.reshape(shp)
    k_out[...] = (k2 * (1.0 + (a - 1.0) * ka_ref[...])).reshape(shp)
    v_out[...] = v
    kk_out[...] = (k2 * kk_ref[...]).reshape(shp)
    a_out[...] = a.reshape(shp)
    g_out[...] = g.reshape(shp)
    del r2


def _rw_pre(z, shift_p, mu_p, w0, w2p, a0, a2p, g2p, k_k, k_a, tokens=256):
    seq, batch, _ = z.shape
    lt, bb = _tiles(seq, batch, tokens)
    d = D_MODEL
    nl = LORA_PAD
    vec = lambda a: a.reshape(1, d)
    full = lambda shape: pl.BlockSpec(shape, lambda b, t: (0,) * len(shape))
    act = pl.BlockSpec((lt, bb, d), lambda b, t: (t, b, 0))
    out_shape = tuple(jax.ShapeDtypeStruct((seq, batch, d), F32) for _ in range(7))
    mu_main = mu_p[:, :, 0:3 * d]
    mu_lora = mu_p[:, :, 3 * d:]
    return pl.pallas_call(
        _rw_pre_kernel,
        grid=(batch // bb, seq // lt),
        in_specs=[pl.BlockSpec((lt, bb, d), lambda b, t: (t, b, COL_R // d)),
                  pl.BlockSpec((lt, bb, d), lambda b, t: (t, b, COL_K // d)),
                  pl.BlockSpec((lt, bb, d), lambda b, t: (t, b, COL_V // d)),
                  pl.BlockSpec((lt, bb, nl), lambda b, t: (t, b, COL_LORA // nl)),
                  pl.BlockSpec((1, bb, 3 * d + nl), lambda b, t: (0, b, 0)),
                  full((1, 1, 3 * d)), full((1, 1, nl)),
                  full((1, d)), full((LANES, d)), full((1, d)), full((LANES, d)), full((2 * LANES, d)),
                  full((1, d)), full((1, d))],
        out_specs=[act] * 7,
        out_shape=out_shape,
        scratch_shapes=[pltpu.VMEM((1, bb, 3 * d), F32), pltpu.VMEM((1, bb, nl), F32)],
        compiler_params=_cp(("arbitrary", "arbitrary")),
    )(z, z, z, z, shift_p, mu_main, mu_lora, vec(w0), w2p, vec(a0), a2p, g2p, vec(k_k), vec(k_a))


def _rw_scan_kernel(r_ref, w_ref, k_ref, v_ref, kk_ref, a_ref, rk_ref, lnw_ref, lnb_ref, s0_ref,
                    o_ref, s_ref, nkk_s, b_s, y_s):
    lt, hd, nl = r_ref.shape

    @pl.when(pl.program_id(1) == 0)
    def _():
        s_ref[...] = s0_ref[...]

    kk = kk_ref[...]
    kkn = kk * lax.rsqrt(jnp.maximum(jnp.sum(kk * kk, axis=1, keepdims=True), 1e-24))
    nkk_s[...] = -kkn
    b_s[...] = kkn * a_ref[...]

    def step(t, carry):
        acc = [jnp.zeros((hd, nl), F32) for _ in range(4)]
        for j in range(hd):
            acc[j % 4] = acc[j % 4] + s_ref[j] * nkk_s[t, pl.ds(j, 1), :]
        sa = (acc[0] + acc[1]) + (acc[2] + acc[3])
        vt = v_ref[t]
        yac = [jnp.zeros((hd, nl), F32) for _ in range(4)]
        for j in range(hd):
            sj = (s_ref[j] * w_ref[t, pl.ds(j, 1), :] + sa * b_s[t, pl.ds(j, 1), :]
                  + vt * k_ref[t, pl.ds(j, 1), :])
            s_ref[j] = sj
            yac[j % 4] = yac[j % 4] + sj * r_ref[t, pl.ds(j, 1), :]
        y_s[t] = (yac[0] + yac[1]) + (yac[2] + yac[3])
        return carry

    lax.fori_loop(0, lt, step, 0)
    ys = y_s[...]
    mean = jnp.mean(ys, axis=1, keepdims=True)
    yc = ys - mean
    var = jnp.mean(yc * yc, axis=1, keepdims=True)
    yn = yc * lax.rsqrt(var + RW_LN_EPS)
    bonus = jnp.sum(r_ref[...] * k_ref[...] * rk_ref[...], axis=1, keepdims=True)
    o_ref[...] = yn * lnw_ref[...] + lnb_ref[...] + bonus * v_ref[...]


def _rw_scan(r, w, k, v, kk, a, rk_t, lnw_t, lnb_t, s0_t, lt=32):
    seq, hd, n = r.shape
    lt = min(lt, seq)
    while seq % lt:
        lt -= 1
    nl = LANES
    act = pl.BlockSpec((lt, hd, nl), lambda c, t: (t, 0, c))
    par = pl.BlockSpec((1, hd, nl), lambda c, t: (0, 0, c))
    st = pl.BlockSpec((hd, hd, nl), lambda c, t: (0, 0, c))
    return pl.pallas_call(
        _rw_scan_kernel,
        grid=(n // nl, seq // lt),
        in_specs=[act] * 6 + [par] * 3 + [st],
        out_specs=[act, st],
        out_shape=(jax.ShapeDtypeStruct((seq, hd, n), F32), jax.ShapeDtypeStruct((hd, hd, n), F32)),
        scratch_shapes=[pltpu.VMEM((lt, hd, nl), F32)] * 3,
        compiler_params=_cp(("arbitrary", "arbitrary")),
    )(r, w, k, v, kk, a, rk_t, lnw_t, lnb_t, s0_t)


def _to_heads_t(x, batch):
    seq = x.shape[0]
    return x.reshape(seq, batch, RW_HEADS, RW_HEAD).transpose(0, 3, 1, 2).reshape(seq, RW_HEAD, batch * RW_HEADS)


def _from_heads_t(x, batch):
    seq = x.shape[0]
    return x.reshape(seq, RW_HEAD, batch, RW_HEADS).transpose(0, 2, 3, 1).reshape(seq, batch, D_MODEL)


def _head_param_t(p, batch):
    t = p.reshape(RW_HEADS, RW_HEAD).T
    return jnp.tile(t[:, None, :], (1, batch, 1)).reshape(1, RW_HEAD, batch * RW_HEADS)


def _merge_kernel(x_ref, ys5_ref, ylru_ref, orw_ref, grw_ref, zg1_ref, zg2_ref, zg3_ref,
                  gt1_ref, sc2_ref, sh2_ref, wglu_ref, bglu_ref, wb1_ref, wb2_ref, wb3_ref, wout_ref,
                  n2g_ref, wr_ref, br_ref,
                  x1_ref, h2_ref, lg_ref):
    lt, bb, d = x_ref.shape
    n = lt * bb
    mm = lambda a, w_ref: jnp.dot(a.astype(BF16), w_ref[...], preferred_element_type=F32)
    flat = lambda ref: ref[...].reshape(n, d)

    y1 = _gelu(flat(ys5_ref))
    y1 = y1 * _sigmoid(mm(y1, wglu_ref) + bglu_ref[...])
    y3 = flat(orw_ref) * flat(grw_ref)
    merged = (_sigmoid(flat(zg1_ref)) * mm(y1, wb1_ref)
              + _sigmoid(flat(zg2_ref)) * mm(flat(ylru_ref), wb2_ref)
              + _sigmoid(flat(zg3_ref)) * mm(y3, wb3_ref))
    upd = mm(merged, wout_ref).reshape(lt, bb, d)
    x1 = x_ref[...] + gt1_ref[...] * upd
    x1_ref[...] = x1
    y = x1 * lax.rsqrt(jnp.mean(x1 * x1, axis=-1, keepdims=True) + RMS_EPS) * n2g_ref[...]
    h2 = y * (1.0 + sc2_ref[...]) + sh2_ref[...]
    h2_ref[...] = h2
    logits = jnp.dot(h2.reshape(n, d), wr_ref[...], precision=HIGHEST, preferred_element_type=F32)
    lg_ref[...] = (logits + br_ref[...]).reshape(lt, bb, lg_ref.shape[-1])


def _merge(x, y_s5, y_lru, o_rw, g_rw, z, mod, w_glu, b_glu, wb1, wb2, wb3, w_out, norm2_g, wr_p, br_p,
           tokens=256):
    seq, batch, d = x.shape
    lt, bb = _tiles(seq, batch, tokens)
    act = pl.BlockSpec((lt, bb, d), lambda b, t: (t, b, 0))
    gate = lambda i: pl.BlockSpec((lt, bb, d), lambda b, t: (t, b, COL_GATE // d + i))
    modspec = lambda i: pl.BlockSpec((1, bb, d), lambda b, t: (0, b, i))
    wspec = pl.BlockSpec((d, d), lambda b, t: (0, 0), pipeline_mode=pl.Buffered(1))
    vspec = pl.BlockSpec((1, d), lambda b, t: (0, 0))
    ne = wr_p.shape[1]
    return pl.pallas_call(
        _merge_kernel,
        grid=(batch // bb, seq // lt),
        in_specs=[act] * 5 + [gate(0), gate(1), gate(2), modspec(2), modspec(4), modspec(3),
                              wspec, vspec, wspec, wspec, wspec, wspec,
                              pl.BlockSpec((1, 1, d), lambda b, t: (0, 0, 0)),
                              pl.BlockSpec((d, ne), lambda b, t: (0, 0)),
                              pl.BlockSpec((1, ne), lambda b, t: (0, 0))],
        out_specs=[act, act, pl.BlockSpec((lt, bb, ne), lambda b, t: (t, b, 0))],
        out_shape=(jax.ShapeDtypeStruct((seq, batch, d), F32),
                   jax.ShapeDtypeStruct((seq, batch, d), F32),
                   jax.ShapeDtypeStruct((seq, batch, ne), F32)),
        compiler_params=_cp(("arbitrary", "arbitrary")),
    )(x, y_s5, y_lru, o_rw, g_rw, z, z, z, mod, mod, mod, w_glu, b_glu.reshape(1, d), wb1, wb2, wb3, w_out,
      norm2_g.reshape(1, 1, d), wr_p, br_p)


ROUTE_IDX, ROUTE_RANK, ROUTE_WT = 0, TOP_K, 2 * TOP_K


def _route_kernel(lg_ref, rec_ref, cnt_ref, run_s):
    tt, nl = lg_ref.shape

    @pl.when(pl.program_id(0) == 0)
    def _():
        run_s[...] = jnp.zeros_like(run_s)

    lane = lax.broadcasted_iota(jnp.int32, (tt, nl), 1).astype(F32)
    neg = jnp.float32(-jnp.inf)
    vals = jnp.where(lane < N_EXPERTS, lg_ref[...], neg)
    tops, hots = [], []
    for _ in range(TOP_K):
        m = jnp.max(vals, axis=-1, keepdims=True)
        idx = jnp.min(jnp.where(vals == m, lane, float(nl)), axis=-1, keepdims=True)
        hot = lane == idx
        vals = jnp.where(hot, neg, vals)
        tops.append((m, idx))
        hots.append(hot)
    es = [jnp.exp(m - tops[0][0]) for m, _ in tops]
    den = es[0]
    for e in es[1:]:
        den = den + e

    mask = jnp.zeros((tt, nl), F32)
    for hot in hots:
        mask = jnp.where(hot, 1.0, mask)
    row = lax.broadcasted_iota(jnp.int32, (tt, tt), 0)
    col = lax.broadcasted_iota(jnp.int32, (tt, tt), 1)
    tri = jnp.where(col < row, 1.0, 0.0).astype(BF16)
    prefix = jnp.dot(tri, mask.astype(BF16), preferred_element_type=F32) + run_s[...]
    run_s[...] = run_s[...] + jnp.sum(mask, axis=0, keepdims=True)
    cnt_ref[...] = run_s[...]

    rec = jnp.zeros((tt, nl), F32)
    for k in range(TOP_K):
        rank = jnp.sum(jnp.where(hots[k], prefix, 0.0), axis=-1, keepdims=True)
        rec = jnp.where(lane == ROUTE_IDX + k, tops[k][1].astype(F32), rec)
        rec = jnp.where(lane == ROUTE_RANK + k, rank, rec)
        rec = jnp.where(lane == ROUTE_WT + k, es[k] / den, rec)
    rec_ref[...] = rec


def _route(logits):
    t, nl = logits.shape
    tt = min(ROUTE_TILE, t)
    while t % tt:
        tt -= SUBLANES
    return pl.pallas_call(
        _route_kernel,
        grid=(t // tt,),
        in_specs=[pl.BlockSpec((tt, nl), lambda i: (i, 0))],
        out_specs=[pl.BlockSpec((tt, nl), lambda i: (i, 0)), pl.BlockSpec((1, nl), lambda i: (0, 0))],
        out_shape=(jax.ShapeDtypeStruct((t, nl), F32), jax.ShapeDtypeStruct((1, nl), F32)),
        scratch_shapes=[pltpu.VMEM((1, nl), F32)],
        compiler_params=_cp(("arbitrary",)),
    )(logits)


def _expert_kernel(te_ref, tv_ref, xs_ref, wgu_ref, bgu_ref, wd_ref, bd_ref, o_ref, wgu_s, wd_s):
    i = pl.program_id(0)
    de = wd_s.shape[0]
    changed = jnp.logical_or(i == 0, te_ref[i] != te_ref[jnp.maximum(i - 1, 0)])

    @pl.when(changed)
    def _():
        wgu_s[...] = wgu_ref[0].astype(BF16)
        wd_s[...] = wd_ref[0].astype(BF16)

    @pl.when(tv_ref[i] > 0)
    def _():
        gu = jnp.dot(xs_ref[...].astype(BF16), wgu_s[...], preferred_element_type=F32) + bgu_ref[0]
        glu = jnp.minimum(gu[:, :de], SWIGLU_LIMIT)
        lin = jnp.clip(gu[:, de:], -SWIGLU_LIMIT, SWIGLU_LIMIT)
        act = glu * _sigmoid(SWIGLU_ALPHA * glu) * (lin + 1.0)
        o_ref[...] = jnp.dot(act.astype(BF16), wd_s[...], preferred_element_type=F32) + bd_ref[0]

    @pl.when(tv_ref[i] == 0)
    def _():
        o_ref[...] = jnp.zeros_like(o_ref)


def _experts(tile_expert, tile_valid, xs, w_gu, b_gu, w_down, b_down):
    rows, d = xs.shape
    ne, _, n2 = w_gu.shape
    de = w_down.shape[1]
    tm = EXPERT_TILE
    grid_spec = pltpu.PrefetchScalarGridSpec(
        num_scalar_prefetch=2,
        grid=(rows // tm,),
        in_specs=[pl.BlockSpec((tm, d), lambda i, te, tv: (i, 0)),
                  pl.BlockSpec((1, d, n2), lambda i, te, tv: (te[i], 0, 0)),
                  pl.BlockSpec((1, 1, n2), lambda i, te, tv: (te[i], 0, 0)),
                  pl.BlockSpec((1, de, d), lambda i, te, tv: (te[i], 0, 0)),
                  pl.BlockSpec((1, 1, d), lambda i, te, tv: (te[i], 0, 0))],
        out_specs=pl.BlockSpec((tm, d), lambda i, te, tv: (i, 0)),
        scratch_shapes=[pltpu.VMEM((d, n2), BF16), pltpu.VMEM((de, d), BF16)])
    return pl.pallas_call(
        _expert_kernel,
        grid_spec=grid_spec,
        out_shape=jax.ShapeDtypeStruct((rows, d), F32),
        compiler_params=_cp(("arbitrary",)),
    )(tile_expert, tile_valid, xs, w_gu, b_gu.reshape(ne, 1, n2), w_down, b_down.reshape(ne, 1, d))


def _combine_kernel(x_ref, og_ref, rec_ref, gt_ref, fg_ref, x2_ref, *, final):
    d = x_ref.shape[-1]
    rec = rec_ref[...]
    y = rec[:, :, ROUTE_WT:ROUTE_WT + 1] * og_ref[:, :, 0:d]
    for k in range(1, TOP_K):
        y = y + rec[:, :, ROUTE_WT + k:ROUTE_WT + k + 1] * og_ref[:, :, k * d:(k + 1) * d]
    x2 = x_ref[...] + gt_ref[...] * y
    if final:
        x2 = x2 * lax.rsqrt(jnp.mean(x2 * x2, axis=-1, keepdims=True) + RMS_EPS) * fg_ref[...]
    x2_ref[...] = x2


def _combine(x1, og_all, tok_off, rec, mod, final_g, final, tokens=256):
    seq, batch, d = x1.shape
    lt, bb = _tiles(seq, batch, tokens)
    nl = rec.shape[-1]
    t_all = og_all.shape[0]
    if t_all % batch == 0 and tok_off % (batch * lt) == 0:
        og = og_all.reshape(t_all // batch, batch, TOP_K * d)
        t0 = tok_off // (batch * lt)
    else:
        og = og_all[tok_off:tok_off + seq * batch].reshape(seq, batch, TOP_K * d)
        t0 = 0
    return pl.pallas_call(
        functools.partial(_combine_kernel, final=final),
        grid=(batch // bb, seq // lt),
        in_specs=[pl.BlockSpec((lt, bb, d), lambda b, t: (t, b, 0)),
                  pl.BlockSpec((lt, bb, TOP_K * d), lambda b, t: (t0 + t, b, 0)),
                  pl.BlockSpec((lt, bb, nl), lambda b, t: (t, b, 0)),
                  pl.BlockSpec((1, bb, d), lambda b, t: (0, b, 5)),
                  pl.BlockSpec((1, 1, d), lambda b, t: (0, 0, 0))],
        out_specs=pl.BlockSpec((lt, bb, d), lambda b, t: (t, b, 0)),
        out_shape=jax.ShapeDtypeStruct((seq, batch, d), F32),
        compiler_params=_cp(("arbitrary", "arbitrary")),
    )(x1, og, rec, mod, final_g.reshape(1, 1, d))


def _pad_in_cols(a, axis):
    d = D_MODEL
    main = lax.slice_in_dim(a, 0, 6 * d, axis=axis)
    lora = lax.slice_in_dim(a, 6 * d, 6 * d + RW_LORA, axis=axis)
    gates = lax.slice_in_dim(a, 6 * d + RW_LORA, 9 * d + RW_LORA, axis=axis)
    pad_shape = list(a.shape)
    pad_shape[axis] = LORA_PAD - RW_LORA
    return jnp.concatenate([main, gates, lora, jnp.zeros(pad_shape, a.dtype)], axis=axis)


def _pad_rw_cols(a):
    pad = jnp.zeros(a.shape[:-1] + (LORA_PAD - RW_LORA,), a.dtype)
    return jnp.concatenate([a, pad], axis=-1)


def _mix_group(x, mod, st, lp, s5p):
    seq, batch, d = x.shape
    s5_re, s5_im, lru_h, lru_conv, rw_s, rw_shift = st
    z = _in_proj(x, mod, lp["norm1_g"], lp["w_in_p"])

    y_s5, n_s5_re, n_s5_im = _s5_mixer(
        z, s5p["ab_re"], s5p["ab_im"], s5p["bb_re"], s5p["bb_im"], s5p["ct_re"], s5p["ct_im"], s5p["d"],
        s5_re.reshape(batch, S5_LANES), s5_im.reshape(batch, S5_LANES))

    y_lru, n_lru_h, tail = _lru_mixer(
        z, lru_conv.transpose(1, 0, 2), lp["lru_conv_w"], lp["lru_conv_b"], lp["lru_w_a"], lp["lru_b_a"],
        lp["lru_w_x"], lp["lru_b_x"], lp["lru_lam"], lru_h)

    shift_p = _pad_rw_cols(rw_shift)[None]
    r, w, k, v, kk, a, g = _rw_pre(z, shift_p, lp["rw_mu_p"], lp["rw_w0"], lp["rw_w2p"], lp["rw_a0"],
                                   lp["rw_a2p"], lp["rw_g2p"], lp["rw_k_k"], lp["rw_k_a"])
    n = batch * RW_HEADS
    s0_t = rw_s.reshape(n, RW_HEAD, RW_HEAD).transpose(2, 1, 0)
    o_t, sn_t = _rw_scan(*(_to_heads_t(t, batch) for t in (r, w, k, v, kk, a)),
                         _head_param_t(lp["rw_r_k"].reshape(-1), batch),
                         _head_param_t(lp["rw_ln_w"], batch), _head_param_t(lp["rw_ln_b"], batch), s0_t)
    o_rw = _from_heads_t(o_t, batch)
    n_rw_s = sn_t.transpose(2, 1, 0).reshape(batch, RW_HEADS, RW_HEAD, RW_HEAD)
    n_rw_shift = jnp.concatenate([z[seq - 1, :, COL_R:COL_R + 3 * d], z[seq - 1, :, COL_LORA:COL_LORA + RW_LORA]],
                                 axis=-1)

    x1, h2, logits = _merge(x, y_s5, y_lru, o_rw, g, z, mod, lp["s5_w_glu"], lp["s5_b_glu"], lp["w_br_s5"],
                            lp["w_br_lru"], lp["w_br_rw"], lp["w_out"], lp["norm2_g"], lp["wr_p"], lp["br_p"])
    new = (n_s5_re.reshape(batch, S5_GROUPS, S5_STATE), n_s5_im.reshape(batch, S5_GROUPS, S5_STATE),
           n_lru_h, tail.transpose(1, 0, 2), n_rw_s, n_rw_shift)
    return x1, h2, logits, new


def _moe(h2_all, logits_all, lp):
    t, d = h2_all.shape
    tm = EXPERT_TILE
    rec, cnt = _route(logits_all)
    idx = rec[:, ROUTE_IDX:ROUTE_IDX + TOP_K].astype(jnp.int32)
    rank = rec[:, ROUTE_RANK:ROUTE_RANK + TOP_K].astype(jnp.int32)
    counts = cnt[0, :N_EXPERTS].astype(jnp.int32)
    padded = ((counts + tm - 1) // tm) * tm
    ends = jnp.cumsum(padded)
    offs = ends - padded
    pos = offs[idx] + rank
    n_tiles = (t * TOP_K + N_EXPERTS * (tm - 1) + tm - 1) // tm
    rows = n_tiles * tm
    src = jnp.zeros((rows,), jnp.int32).at[pos.reshape(-1)].set(
        jnp.repeat(jnp.arange(t, dtype=jnp.int32), TOP_K))
    starts = jnp.arange(n_tiles, dtype=jnp.int32) * tm
    tile_valid = (starts < ends[-1]).astype(jnp.int32)
    owner = lambda row: jnp.sum((ends[None, :] <= row[:, None]).astype(jnp.int32), axis=1)
    last = owner(ends[-1:] - 1)[0]
    tile_expert = jnp.clip(jnp.where(tile_valid > 0, owner(starts), last), 0, N_EXPERTS - 1)
    xs = h2_all.at[src].get(mode="promise_in_bounds")
    os_ = _experts(tile_expert, tile_valid, xs, lp["moe_w_gu"], lp["moe_b_gu"], lp["moe_w_down"], lp["moe_b_down"])
    og = os_.at[pos.reshape(-1)].get(mode="promise_in_bounds").reshape(t, TOP_K * d)
    return og, rec


def kernel(x_prompt, x_sample, state_s5_re, state_s5_im, state_lru_h, cache_lru_conv, state_rwkv, cache_rwkv_shift, c_prompt, c_sample, w_mod, b_mod, norm1_g, w_in, s5_lam_re, s5_lam_im, s5_log_step, s5_b_re, s5_b_im, s5_c_re, s5_c_im, s5_d, s5_w_glu, s5_b_glu, lru_conv_w, lru_conv_b, lru_w_a, lru_b_a, lru_w_x, lru_b_x, lru_lam, rw_mu, rw_w0, rw_w2, rw_a0, rw_a2, rw_g2, rw_k_k, rw_k_a, rw_r_k, rw_ln_w, rw_ln_b, w_br_s5, w_br_lru, w_br_rw, w_out, norm2_g, moe_w_router, moe_b_router, moe_w_gu, moe_b_gu, moe_w_down, moe_b_down, final_g):
    depth = w_mod.shape[0]
    d = D_MODEL
    bp, lp_len = x_prompt.shape[0], x_prompt.shape[1]
    bs, ls_len = x_sample.shape[0], x_sample.shape[1]

    mod_all = _modulation(jnp.concatenate([c_prompt, c_sample], axis=0), w_mod, b_mod)

    ab_re, ab_im, q_re, q_im = _s5_params(s5_lam_re, s5_lam_im, s5_log_step)
    shp = (depth, S5_GROUPS, S5_STATE)
    q_re, q_im = q_re.reshape(shp)[..., None], q_im.reshape(shp)[..., None]
    bbar_re = q_re * s5_b_re - q_im * s5_b_im
    bbar_im = q_re * s5_b_im + q_im * s5_b_re
    per = LANES // S5_GROUP

    xs = [x_prompt.transpose(1, 0, 2), x_sample.transpose(1, 0, 2)]
    zeros_like_state = lambda s, b: jnp.zeros((b,) + s.shape[2:], F32)
    sample_states = (state_s5_re, state_s5_im, state_lru_h, cache_lru_conv, state_rwkv, cache_rwkv_shift)
    collected = [tuple([] for _ in sample_states), tuple([] for _ in sample_states)]

    for l in range(depth):
        lp = dict(
            norm1_g=norm1_g[l], w_in_p=_pad_in_cols(w_in[l], 1).astype(BF16),
            lru_conv_w=lru_conv_w[l], lru_conv_b=lru_conv_b[l], lru_w_a=lru_w_a[l], lru_b_a=lru_b_a[l],
            lru_w_x=lru_w_x[l], lru_b_x=lru_b_x[l], lru_lam=lru_lam[l],
            rw_mu_p=_pad_rw_cols(rw_mu[l]).reshape(1, 1, -1), rw_w0=rw_w0[l], rw_a0=rw_a0[l],
            rw_w2p=jnp.concatenate([rw_w2[l], jnp.zeros((LANES - RW_W_LORA, d), F32)], axis=0).astype(BF16),
            rw_a2p=jnp.concatenate([jnp.zeros((RW_W_LORA, d), F32), rw_a2[l]], axis=0).astype(BF16),
            rw_g2p=jnp.concatenate([rw_g2[l], jnp.zeros((2 * LANES - RW_G_LORA, d), F32)], axis=0).astype(BF16),
            rw_k_k=rw_k_k[l], rw_k_a=rw_k_a[l], rw_r_k=rw_r_k[l], rw_ln_w=rw_ln_w[l], rw_ln_b=rw_ln_b[l],
            s5_w_glu=s5_w_glu[l].astype(BF16), s5_b_glu=s5_b_glu[l], w_br_s5=w_br_s5[l].astype(BF16),
            w_br_lru=w_br_lru[l].astype(BF16), w_br_rw=w_br_rw[l].astype(BF16), w_out=w_out[l].astype(BF16),
            norm2_g=norm2_g[l],
            wr_p=jnp.concatenate([moe_w_router[l], jnp.zeros((d, LANES - N_EXPERTS), F32)], axis=1),
            br_p=jnp.concatenate([moe_b_router[l], jnp.zeros((LANES - N_EXPERTS,), F32)]).reshape(1, LANES),
            moe_w_gu=moe_w_gu[l], moe_b_gu=moe_b_gu[l], moe_w_down=moe_w_down[l], moe_b_down=moe_b_down[l])
        g0 = l * S5_GROUPS
        s5p = dict(
            ab_re=ab_re[g0:g0 + S5_GROUPS].reshape(1, S5_LANES), ab_im=ab_im[g0:g0 + S5_GROUPS].reshape(1, S5_LANES),
            bb_re=_block_diag(bbar_re[l].transpose(0, 2, 1), per).astype(BF16),
            bb_im=_block_diag(bbar_im[l].transpose(0, 2, 1), per).astype(BF16),
            ct_re=_block_diag(s5_c_re[l].transpose(0, 2, 1), per).astype(BF16),
            ct_im=_block_diag(s5_c_im[l].transpose(0, 2, 1), per).astype(BF16),
            d=s5_d[l].reshape(1, d))

        mods = [mod_all[l, :bp][None], mod_all[l, bp:][None]]
        states = [tuple(zeros_like_state(s, bp) for s in sample_states), tuple(s[l] for s in sample_states)]
        x1s, h2s, lgs = [], [], []
        for gi in range(2):
            x1, h2, lg, new = _mix_group(xs[gi], mods[gi], states[gi], lp, s5p)
            x1s.append(x1)
            h2s.append(h2.reshape(-1, d))
            lgs.append(lg.reshape(-1, LANES))
            for lst, s in zip(collected[gi], new):
                lst.append(s)

        og, rec = _moe(jnp.concatenate(h2s[::-1], axis=0), jnp.concatenate(lgs[::-1], axis=0), lp)
        ts = ls_len * bs
        offs = [ts, 0]
        recs = [rec[ts:].reshape(lp_len, bp, LANES), rec[:ts].reshape(ls_len, bs, LANES)]
        xs = [_combine(x1s[gi], og, offs[gi], recs[gi], mods[gi], final_g, final=(l == depth - 1))
              for gi in range(2)]

    y_prompt = xs[0].transpose(1, 0, 2)
    y_sample = xs[1].transpose(1, 0, 2)
    p_states = tuple(jnp.stack(lst) for lst in collected[0])
    s_states = tuple(jnp.stack(lst) for lst in collected[1])
    return (y_prompt, y_sample) + p_states + s_states
```

```python
import functools
import math

import jax
import jax.numpy as jnp
from jax import lax
from jax.experimental import pallas as pl
from jax.experimental.pallas import tpu as pltpu

F32 = jnp.float32
BF16 = jnp.bfloat16
HIGHEST = lax.Precision.HIGHEST

D_MODEL = 1024
RMS_EPS = 1e-5
S5_GROUP = 16
S5_GROUPS = D_MODEL // S5_GROUP
S5_STATE = 64
S5_LANES = S5_GROUPS * S5_STATE
LRU_HEADS = 8
LRU_BLOCK = D_MODEL // LRU_HEADS
CONV_WIDTH = 4
LRU_C = 8.0
RW_HEAD = 64
RW_HEADS = D_MODEL // RW_HEAD
RW_W_LORA = 64
RW_A_LORA = 64
RW_G_LORA = 160
RW_LORA = RW_W_LORA + RW_A_LORA + RW_G_LORA
RW_COLS = 3 * D_MODEL + RW_LORA
RW_LN_EPS = 64e-5
N_EXPERTS = 32
TOP_K = 4
SWIGLU_ALPHA = 1.702
SWIGLU_LIMIT = 7.0

LANES = 128
SUBLANES = 8
VMEM_LIMIT = 56 * 1024 * 1024

LORA_PAD = 512
COL_S5 = 0
COL_XLRU = 1 * D_MODEL
COL_GLRU = 2 * D_MODEL
COL_R = 3 * D_MODEL
COL_K = 4 * D_MODEL
COL_V = 5 * D_MODEL
COL_GATE = 6 * D_MODEL
COL_LORA = 9 * D_MODEL
D_IN_PAD = COL_LORA + LORA_PAD

S5_LANE_BLOCK = LANES
S5_STATE_BLOCK = (LANES // S5_GROUP) * S5_STATE
EXPERT_TILE = 512
ROUTE_TILE = 512


def _cp(sem, vmem=VMEM_LIMIT):
    return pltpu.CompilerParams(dimension_semantics=sem, vmem_limit_bytes=vmem)


def _gelu(x):
    return 0.5 * x * (1.0 + jnp.tanh(math.sqrt(2.0 / math.pi) * (x + 0.044715 * (x * x * x))))


def _sigmoid(x):
    return 1.0 / (1.0 + jnp.exp(-x))


def _softplus(x):
    return jnp.maximum(x, 0.0) + jnp.log1p(jnp.exp(-jnp.abs(x)))


def _tiles(seq, batch, tokens):
    bb = min(batch, max(SUBLANES, (tokens // seq) // SUBLANES * SUBLANES))
    while batch % bb:
        bb -= SUBLANES
    lt = max(1, min(seq, tokens // bb))
    while seq % lt:
        lt -= 1
    return lt, bb


def _mod_kernel(c_ref, w_ref, b_ref, o_ref):
    c = c_ref[...]
    s = c * _sigmoid(c)
    o_ref[0] = jnp.dot(s, w_ref[0], precision=HIGHEST, preferred_element_type=F32) + b_ref[0]


def _modulation(c, w_mod, b_mod):
    depth, d, n = w_mod.shape
    bc = c.shape[0]
    tn = 1536
    return pl.pallas_call(
        _mod_kernel, name="adaln_mod",
        grid=(depth, n // tn),
        in_specs=[pl.BlockSpec((bc, d), lambda l, j: (0, 0)),
                  pl.BlockSpec((1, d, tn), lambda l, j: (l, 0, j)),
                  pl.BlockSpec((1, 1, tn), lambda l, j: (l, 0, j))],
        out_specs=pl.BlockSpec((1, bc, tn), lambda l, j: (l, 0, j)),
        out_shape=jax.ShapeDtypeStruct((depth, bc, n), F32),
        compiler_params=_cp(("arbitrary", "arbitrary")),
    )(c, w_mod, b_mod.reshape(depth, 1, n))


def _in_kernel(x_ref, sh_ref, sc_ref, g_ref, w_ref, o_ref, h_scr):
    lt, bb, d = x_ref.shape

    @pl.when(pl.program_id(2) == 0)
    def _():
        x = x_ref[...]
        y = x * lax.rsqrt(jnp.mean(x * x, axis=-1, keepdims=True) + RMS_EPS) * g_ref[...]
        h = y * (1.0 + sc_ref[...]) + sh_ref[...]
        h_scr[...] = h.reshape(lt * bb, d).astype(BF16)

    o = jnp.dot(h_scr[...], w_ref[...], preferred_element_type=F32)
    o_ref[...] = o.reshape(lt, bb, o.shape[-1])


def _in_proj(x, mod, norm_g, w_in_p, tokens=1024, tn=512):
    seq, batch, d = x.shape
    lt, bb = _tiles(seq, batch, tokens)
    n = w_in_p.shape[1]
    return pl.pallas_call(
        _in_kernel, name="in_proj",
        grid=(batch // bb, seq // lt, n // tn),
        in_specs=[pl.BlockSpec((lt, bb, d), lambda b, t, j: (t, b, 0)),
                  pl.BlockSpec((1, bb, d), lambda b, t, j: (0, b, 0)),
                  pl.BlockSpec((1, bb, d), lambda b, t, j: (0, b, 1)),
                  pl.BlockSpec((1, 1, d), lambda b, t, j: (0, 0, 0)),
                  pl.BlockSpec((d, tn), lambda b, t, j: (0, j))],
        out_specs=pl.BlockSpec((lt, bb, tn), lambda b, t, j: (t, b, j)),
        out_shape=jax.ShapeDtypeStruct((seq, batch, n), F32),
        scratch_shapes=[pltpu.VMEM((lt * bb, d), BF16)],
        compiler_params=_cp(("arbitrary", "arbitrary", "arbitrary")),
    )(x, mod, mod, norm_g.reshape(1, 1, d), w_in_p)


def _s5_kernel(u_ref, bbr_ref, bbi_ref, ar_ref, ai_ref, ctr_ref, cti_ref, d_ref, h0r_ref, h0i_ref,
               y_ref, hr_ref, hi_ref, sr, si):
    lt, bb, nl = u_ref.shape
    ns = sr.shape[-1]

    @pl.when(pl.program_id(2) == 0)
    def _():
        hr_ref[...] = h0r_ref[...]
        hi_ref[...] = h0i_ref[...]

    u2 = u_ref[...].reshape(lt * bb, nl)
    ub = u2.astype(BF16)
    sr[...] = jnp.dot(ub, bbr_ref[0], preferred_element_type=F32).reshape(lt, bb, ns)
    si[...] = jnp.dot(ub, bbi_ref[0], preferred_element_type=F32).reshape(lt, bb, ns)
    ar = jnp.broadcast_to(ar_ref[...], (bb, ns))
    ai = jnp.broadcast_to(ai_ref[...], (bb, ns))

    def body(i, carry):
        hr, hi = carry
        nr = ar * hr - ai * hi + sr[i]
        ni = ar * hi + ai * hr + si[i]
        sr[i] = nr
        si[i] = ni
        return nr, ni

    hr, hi = lax.fori_loop(0, lt, body, (hr_ref[...], hi_ref[...]), unroll=min(lt, 8))
    hr_ref[...] = hr
    hi_ref[...] = hi
    y = (jnp.dot(sr[...].reshape(lt * bb, ns).astype(BF16), ctr_ref[0], preferred_element_type=F32)
         - jnp.dot(si[...].reshape(lt * bb, ns).astype(BF16), cti_ref[0], preferred_element_type=F32)
         + d_ref[...] * u2)
    y_ref[...] = y.reshape(lt, bb, nl)


def _s5_params_kernel(lr_ref, li_ref, ls_ref, abr_ref, abi_ref, qr_ref, qi_ref):
    lr = lr_ref[...]
    li = li_ref[...]
    step = jnp.exp(ls_ref[...])
    mag = jnp.exp(lr * step)
    ab_re = mag * jnp.cos(li * step)
    ab_im = mag * jnp.sin(li * step)
    den = lr * lr + li * li
    abr_ref[...] = ab_re
    abi_ref[...] = ab_im
    qr_ref[...] = ((ab_re - 1.0) * lr + ab_im * li) / den
    qi_ref[...] = (ab_im * lr - (ab_re - 1.0) * li) / den


def _s5_params(lam_re, lam_im, log_step):
    depth, g, n = lam_re.shape
    shp = jax.ShapeDtypeStruct((depth * g, n), F32)
    return pl.pallas_call(_s5_params_kernel, name="s5_params", out_shape=(shp, shp, shp, shp))(
        lam_re.reshape(depth * g, n), lam_im.reshape(depth * g, n), log_step.reshape(depth * g, 1))


def _block_diag(blocks, per):
    g, a, b = blocks.shape
    x = blocks.reshape(g // per, per, a, b)
    eye = jnp.eye(per, dtype=blocks.dtype)
    return jnp.einsum("gpab,pq->gpaqb", x, eye).reshape(g // per, per * a, per * b)


def _s5_mixer(z, ab_re, ab_im, bb_re, bb_im, ct_re, ct_im, d_skip, h0_re, h0_im, tokens=512):
    seq, batch, _ = z.shape
    lt, bb = _tiles(seq, batch, tokens)
    nl, ns = S5_LANE_BLOCK, S5_STATE_BLOCK
    nblk = D_MODEL // nl
    col0 = COL_S5 // nl
    y, hr, hi = pl.pallas_call(
        _s5_kernel, name="s5_mixer",
        grid=(batch // bb, nblk, seq // lt),
        in_specs=[pl.BlockSpec((lt, bb, nl), lambda b, j, t: (t, b, col0 + j)),
                  pl.BlockSpec((1, nl, ns), lambda b, j, t: (j, 0, 0)),
                  pl.BlockSpec((1, nl, ns), lambda b, j, t: (j, 0, 0)),
                  pl.BlockSpec((1, ns), lambda b, j, t: (0, j)),
                  pl.BlockSpec((1, ns), lambda b, j, t: (0, j)),
                  pl.BlockSpec((1, ns, nl), lambda b, j, t: (j, 0, 0)),
                  pl.BlockSpec((1, ns, nl), lambda b, j, t: (j, 0, 0)),
                  pl.BlockSpec((1, nl), lambda b, j, t: (0, j)),
                  pl.BlockSpec((bb, ns), lambda b, j, t: (b, j)),
                  pl.BlockSpec((bb, ns), lambda b, j, t: (b, j))],
        out_specs=[pl.BlockSpec((lt, bb, nl), lambda b, j, t: (t, b, j)),
                   pl.BlockSpec((bb, ns), lambda b, j, t: (b, j)),
                   pl.BlockSpec((bb, ns), lambda b, j, t: (b, j))],
        out_shape=(jax.ShapeDtypeStruct((seq, batch, D_MODEL), F32),
                   jax.ShapeDtypeStruct((batch, S5_LANES), F32),
                   jax.ShapeDtypeStruct((batch, S5_LANES), F32)),
        scratch_shapes=[pltpu.VMEM((lt, bb, ns), F32), pltpu.VMEM((lt, bb, ns), F32)],
        compiler_params=_cp(("arbitrary", "arbitrary", "arbitrary")),
    )(z, bb_re, bb_im, ab_re, ab_im, ct_re, ct_im, d_skip, h0_re, h0_im)
    return y, hr, hi


def _lru_kernel(x_ref, g_ref, cbuf_ref, cw_ref, cb_ref, wa_ref, ba_ref, wx_ref, bx_ref, lam_ref, h0_ref,
                y_ref, hn_ref, tail_ref, prev_s, a_s, h_s):
    lt, bb, w = x_ref.shape
    taps = cw_ref.shape[0]

    @pl.when(pl.program_id(2) == 0)
    def _():
        hn_ref[...] = h0_ref[...]
        prev_s[...] = cbuf_ref[...]

    xp = jnp.concatenate([prev_s[...], x_ref[...]], axis=0)
    xc = cb_ref[...] + xp[0:lt] * cw_ref[0]
    for tap in range(1, taps):
        xc = xc + xp[tap:tap + lt] * cw_ref[tap]
    prev_s[...] = xp[lt:lt + taps - 1]
    tail_ref[...] = xp[lt:lt + taps - 1]

    xc2 = xc.reshape(lt * bb, w)
    xb = xc2.astype(BF16)
    r = _sigmoid(jnp.dot(xb, wa_ref[0], preferred_element_type=F32) + ba_ref[...])
    i = _sigmoid(jnp.dot(xb, wx_ref[0], preferred_element_type=F32) + bx_ref[...])
    log_a = (-LRU_C) * r * _softplus(-lam_ref[...])
    a = jnp.exp(log_a)
    th = jnp.tanh(log_a)
    neg_expm1 = -2.0 * th / (1.0 - th)
    b = jnp.sqrt(neg_expm1) * (i * xc2)
    a_s[...] = a.reshape(lt, bb, w)
    h_s[...] = b.reshape(lt, bb, w)

    def body(t, h):
        h = a_s[t] * h + h_s[t]
        h_s[t] = h
        return h

    hn_ref[...] = lax.fori_loop(0, lt, body, hn_ref[...], unroll=min(lt, 8))
    y_ref[...] = h_s[...] * _gelu(g_ref[...])


def _lru_mixer(z, conv_buf_t, conv_w, conv_b, w_a, b_a, w_x, b_x, lam, h0, tokens=1024):
    seq, batch, _ = z.shape
    lt, bb = _tiles(seq, batch, tokens)
    w = LRU_BLOCK
    xcol, gcol = COL_XLRU // w, COL_GLRU // w
    taps = CONV_WIDTH
    vec = lambda a: a.reshape(1, D_MODEL)
    vspec = pl.BlockSpec((1, w), lambda b, h, t: (0, h))
    y, hn, tail = pl.pallas_call(
        _lru_kernel, name="rglru_mixer",
        grid=(batch // bb, LRU_HEADS, seq // lt),
        in_specs=[pl.BlockSpec((lt, bb, w), lambda b, h, t: (t, b, xcol + h)),
                  pl.BlockSpec((lt, bb, w), lambda b, h, t: (t, b, gcol + h)),
                  pl.BlockSpec((taps - 1, bb, w), lambda b, h, t: (0, b, h)),
                  pl.BlockSpec((taps, 1, w), lambda b, h, t: (0, 0, h)),
                  vspec,
                  pl.BlockSpec((1, w, w), lambda b, h, t: (h, 0, 0)),
                  vspec,
                  pl.BlockSpec((1, w, w), lambda b, h, t: (h, 0, 0)),
                  vspec, vspec,
                  pl.BlockSpec((bb, w), lambda b, h, t: (b, h))],
        out_specs=[pl.BlockSpec((lt, bb, w), lambda b, h, t: (t, b, h)),
                   pl.BlockSpec((bb, w), lambda b, h, t: (b, h)),
                   pl.BlockSpec((taps - 1, bb, w), lambda b, h, t: (0, b, h))],
        out_shape=(jax.ShapeDtypeStruct((seq, batch, D_MODEL), F32),
                   jax.ShapeDtypeStruct((batch, D_MODEL), F32),
                   jax.ShapeDtypeStruct((taps - 1, batch, D_MODEL), F32)),
        scratch_shapes=[pltpu.VMEM((taps - 1, bb, w), F32), pltpu.VMEM((lt, bb, w), F32),
                        pltpu.VMEM((lt, bb, w), F32)],
        compiler_params=_cp(("arbitrary", "arbitrary", "arbitrary")),
    )(z, z, conv_buf_t, conv_w.reshape(taps, 1, D_MODEL), vec(conv_b), w_a.astype(BF16), vec(b_a),
      w_x.astype(BF16), vec(b_x), vec(lam), h0)
    return y, hn, tail


def _rwkv_kernel(zr_ref, zk_ref, zv_ref, zl_ref, sh_ref, shl_ref, mu_ref, mul_ref, w0_ref, w2_ref, a0_ref, a2_ref,
                 g2_ref, kk_ref, ka_ref, rk_ref, lnw_ref, lnb_ref, s0_ref,
                 y_ref, sn_ref,
                 prev_s, prevl_s, t_scr, s_scr, r_s, w_s, k_s, v_s, nkk_s, b_s, y_s):
    lt, bb, wd = zr_ref.shape
    kp = wd // LANES
    hd = RW_HEAD
    nl = zl_ref.shape[-1]
    n = lt * bb
    ti = pl.program_id(2)

    @pl.when(ti == 0)
    def _():
        prev_s[...] = sh_ref[...]
        prevl_s[...] = shl_ref[...]
        t_scr[...] = s0_ref[0, 0].T.reshape(hd, hd, LANES)
        for j in range(hd):
            s_scr[j] = t_scr[:, j, :]

    def shifted(z_ref, prev, mu):
        z = z_ref[...]
        zp = jnp.concatenate([prev, z[0:lt - 1]], axis=0) if lt > 1 else prev
        return z + (zp - z) * mu, z[lt - 1:lt]

    r, last_r = shifted(zr_ref, prev_s[0:1], mu_ref[0:1])
    k, last_k = shifted(zk_ref, prev_s[1:2], mu_ref[1:2])
    v, last_v = shifted(zv_ref, prev_s[2:3], mu_ref[2:3])
    lo, last_l = shifted(zl_ref, prevl_s[...], mul_ref[...])
    prev_s[0:1] = last_r
    prev_s[1:2] = last_k
    prev_s[2:3] = last_v
    prevl_s[...] = last_l

    lo2 = lo.reshape(n, nl)
    wa_in = lo2[:, 0:LANES]
    g_in = lo2[:, LANES:3 * LANES]
    wpre = w0_ref[...] + jnp.dot(jnp.tanh(wa_in).astype(BF16), w2_ref[...], preferred_element_type=F32)
    decay = jnp.exp(-jnp.exp(-_softplus(-wpre) - 0.5))
    a = _sigmoid(a0_ref[...] + jnp.dot(wa_in.astype(BF16), a2_ref[...], preferred_element_type=F32))
    g = jnp.dot(_sigmoid(g_in).astype(BF16), g2_ref[...], preferred_element_type=F32)
    k2 = k.reshape(n, wd)
    shp = (lt, bb, wd)

    def to_lanes(q):
        rows = jnp.concatenate([q[:, :, c * LANES:(c + 1) * LANES] for c in range(kp)], axis=1) if kp > 1 else q
        t = jnp.swapaxes(rows, 1, 2)
        return jnp.concatenate([t[:, 0:hd, :], t[:, hd:2 * hd, :]], axis=2)

    r_s[...] = to_lanes(r)
    w_s[...] = to_lanes(decay.reshape(shp))
    k_s[...] = to_lanes((k2 * (1.0 + (a - 1.0) * ka_ref[...])).reshape(shp))
    v_s[...] = to_lanes(v)
    kk = to_lanes((k2 * kk_ref[...]).reshape(shp))
    kkn = kk * lax.rsqrt(jnp.maximum(jnp.sum(kk * kk, axis=1, keepdims=True), 1e-24))
    nkk_s[...] = -kkn
    b_s[...] = kkn * to_lanes(a.reshape(shp))

    def step(t, carry):
        acc = [jnp.zeros((hd, LANES), F32) for _ in range(4)]
        for j in range(hd):
            acc[j % 4] = acc[j % 4] + s_scr[j] * nkk_s[t, pl.ds(j, 1), :]
        sa = (acc[0] + acc[1]) + (acc[2] + acc[3])
        vt = v_s[t]
        yac = [jnp.zeros((hd, LANES), F32) for _ in range(4)]
        for j in range(hd):
            sj = (s_scr[j] * w_s[t, pl.ds(j, 1), :] + sa * b_s[t, pl.ds(j, 1), :]
                  + vt * k_s[t, pl.ds(j, 1), :])
            s_scr[j] = sj
            yac[j % 4] = yac[j % 4] + sj * r_s[t, pl.ds(j, 1), :]
        y_s[t] = (yac[0] + yac[1]) + (yac[2] + yac[3])
        return carry

    lax.fori_loop(0, lt, step, 0)

    ys = y_s[...]
    mean = jnp.mean(ys, axis=1, keepdims=True)
    yc = ys - mean
    var = jnp.mean(yc * yc, axis=1, keepdims=True)
    yn = yc * lax.rsqrt(var + RW_LN_EPS)
    bonus = jnp.sum(r_s[...] * k_s[...] * rk_ref[...], axis=1, keepdims=True)
    o = yn * lnw_ref[...] + lnb_ref[...] + bonus * v_s[...]
    o = jnp.swapaxes(jnp.concatenate([o[:, :, 0:hd], o[:, :, hd:2 * hd]], axis=1), 1, 2)
    nat = jnp.concatenate([o[:, c * bb:(c + 1) * bb, :] for c in range(kp)], axis=2) if kp > 1 else o
    y_ref[...] = nat * g.reshape(shp)

    @pl.when(ti == pl.num_programs(2) - 1)
    def _():
        for j in range(hd):
            t_scr[:, j, :] = s_scr[j]
        sn_ref[0, 0] = t_scr[...].reshape(hd * hd, LANES).T


def _rwkv(z, rw_shift, rw_s, lp, lt=32):
    seq, batch, _ = z.shape
    d, hd, nl = D_MODEL, RW_HEAD, LORA_PAD
    bb, kp = _rw_tiles(batch)
    wd = LANES * kp
    n_pg = d // wd
    lt = min(lt, seq)
    while seq % lt:
        lt -= 1
    pad = lambda a: jnp.concatenate([a, jnp.zeros(a.shape[:-1] + (nl - RW_LORA,), a.dtype)], axis=-1)
    sh = rw_shift[:, 0:3 * d].reshape(batch, 3, d).transpose(1, 0, 2)
    shl = pad(rw_shift[:, 3 * d:])[None]
    mu = lp["rw_mu"][0:3 * d].reshape(3, 1, d)
    mul = pad(lp["rw_mu"][3 * d:]).reshape(1, 1, nl)
    vec = lambda a: a.reshape(1, d)
    zspec = lambda col: pl.BlockSpec((lt, bb, wd), lambda b, g, t: (t, b, col // wd + g))
    vspec = pl.BlockSpec((1, wd), lambda b, g, t: (0, g))
    wspec = lambda rows: pl.BlockSpec((rows, wd), lambda b, g, t: (0, g))
    pspec = pl.BlockSpec((1, hd, LANES), lambda b, g, t: (g, 0, 0))
    sspec = pl.BlockSpec((1, 1, LANES, hd * hd), lambda b, g, t: (b, g, 0, 0))
    chunk = pltpu.VMEM((lt, hd, LANES), F32)
    y, sn = pl.pallas_call(
        _rwkv_kernel, name="rwkv7_mixer",
        grid=(batch // bb, n_pg, seq // lt),
        in_specs=[zspec(COL_R), zspec(COL_K), zspec(COL_V),
                  pl.BlockSpec((lt, bb, nl), lambda b, g, t: (t, b, COL_LORA // nl)),
                  pl.BlockSpec((3, bb, wd), lambda b, g, t: (0, b, g)),
                  pl.BlockSpec((1, bb, nl), lambda b, g, t: (0, b, 0)),
                  pl.BlockSpec((3, 1, wd), lambda b, g, t: (0, 0, g)),
                  pl.BlockSpec((1, 1, nl), lambda b, g, t: (0, 0, 0)),
                  vspec, wspec(LANES), vspec, wspec(LANES), wspec(2 * LANES), vspec, vspec,
                  pspec, pspec, pspec, sspec],
        out_specs=[pl.BlockSpec((lt, bb, wd), lambda b, g, t: (t, b, g)), sspec],
        out_shape=(jax.ShapeDtypeStruct((seq, batch, d), F32),
                   jax.ShapeDtypeStruct((batch // bb, n_pg, LANES, hd * hd), F32)),
        scratch_shapes=[pltpu.VMEM((3, bb, wd), F32), pltpu.VMEM((1, bb, nl), F32),
                        pltpu.VMEM((hd, hd, LANES), F32), pltpu.VMEM((hd, hd, LANES), F32)] + [chunk] * 7,
        compiler_params=_cp(("arbitrary", "arbitrary", "arbitrary")),
    )(z, z, z, z, sh, shl, mu, mul, vec(lp["rw_w0"]), lp["rw_w2p"], vec(lp["rw_a0"]), lp["rw_a2p"], lp["rw_g2p"],
      vec(lp["rw_k_k"]), vec(lp["rw_k_a"]), _rw_param_blocks(lp["rw_r_k"].reshape(-1), bb, kp),
      _rw_param_blocks(lp["rw_ln_w"], bb, kp), _rw_param_blocks(lp["rw_ln_b"], bb, kp),
      _rw_state_to_blocks(rw_s, bb, kp))
    return y, _rw_state_from_blocks(sn, bb, kp)


def _rw_tiles(batch):
    bb = min(batch, LANES // 2)
    assert (LANES // 2) % bb == 0 and batch % bb == 0 and bb % SUBLANES == 0, batch
    return bb, (LANES // 2) // bb


def _rw_state_to_blocks(s, bb, kp):
    b = s.shape[0]
    n_pg = RW_HEADS // (2 * kp)
    x = s.reshape(b // bb, bb, n_pg, kp, 2, RW_HEAD * RW_HEAD)
    return x.transpose(0, 2, 4, 3, 1, 5).reshape(b // bb, n_pg, LANES, RW_HEAD * RW_HEAD)


def _rw_state_from_blocks(x, bb, kp):
    nb, n_pg = x.shape[0], x.shape[1]
    x = x.reshape(nb, n_pg, 2, kp, bb, RW_HEAD * RW_HEAD).transpose(0, 4, 1, 3, 2, 5)
    return x.reshape(nb * bb, RW_HEADS, RW_HEAD, RW_HEAD)


def _rw_param_blocks(p, bb, kp):
    n_pg = RW_HEADS // (2 * kp)
    x = p.reshape(n_pg, kp, 2, RW_HEAD).transpose(0, 3, 2, 1)
    return jnp.broadcast_to(x[..., None], (n_pg, RW_HEAD, 2, kp, bb)).reshape(n_pg, RW_HEAD, LANES)


def _merge_kernel(x_ref, ys5_ref, ylru_ref, yrw_ref, zg1_ref, zg2_ref, zg3_ref,
                  gt1_ref, sc2_ref, sh2_ref, wglu_ref, bglu_ref, wb1_ref, wb2_ref, wb3_ref, wout_ref,
                  n2g_ref, wr_ref, br_ref,
                  x1_ref, h2_ref, lg_ref):
    lt, bb, d = x_ref.shape
    n = lt * bb
    mm = lambda a, w_ref: jnp.dot(a.astype(BF16), w_ref[...], preferred_element_type=F32)
    flat = lambda ref: ref[...].reshape(n, d)

    y1 = _gelu(flat(ys5_ref))
    y1 = y1 * _sigmoid(mm(y1, wglu_ref) + bglu_ref[...])
    y3 = flat(yrw_ref)
    merged = (_sigmoid(flat(zg1_ref)) * mm(y1, wb1_ref)
              + _sigmoid(flat(zg2_ref)) * mm(flat(ylru_ref), wb2_ref)
              + _sigmoid(flat(zg3_ref)) * mm(y3, wb3_ref))
    upd = mm(merged, wout_ref).reshape(lt, bb, d)
    x1 = x_ref[...] + gt1_ref[...] * upd
    x1_ref[...] = x1
    y = x1 * lax.rsqrt(jnp.mean(x1 * x1, axis=-1, keepdims=True) + RMS_EPS) * n2g_ref[...]
    h2 = y * (1.0 + sc2_ref[...]) + sh2_ref[...]
    h2_ref[...] = h2
    logits = jnp.dot(h2.reshape(n, d), wr_ref[...], precision=HIGHEST, preferred_element_type=F32)
    lg_ref[...] = (logits + br_ref[...]).reshape(lt, bb, lg_ref.shape[-1])


def _merge(x, y_s5, y_lru, y_rw, z, mod, w_glu, b_glu, wb1, wb2, wb3, w_out, norm2_g, wr_p, br_p,
           tokens=256):
    seq, batch, d = x.shape
    lt, bb = _tiles(seq, batch, tokens)
    act = pl.BlockSpec((lt, bb, d), lambda b, t: (t, b, 0))
    gate = lambda i: pl.BlockSpec((lt, bb, d), lambda b, t: (t, b, COL_GATE // d + i))
    modspec = lambda i: pl.BlockSpec((1, bb, d), lambda b, t: (0, b, i))
    wspec = pl.BlockSpec((d, d), lambda b, t: (0, 0), pipeline_mode=pl.Buffered(1))
    vspec = pl.BlockSpec((1, d), lambda b, t: (0, 0))
    ne = wr_p.shape[1]
    return pl.pallas_call(
        _merge_kernel, name="merge_norm2_router",
        grid=(batch // bb, seq // lt),
        in_specs=[act] * 4 + [gate(0), gate(1), gate(2), modspec(2), modspec(4), modspec(3),
                              wspec, vspec, wspec, wspec, wspec, wspec,
                              pl.BlockSpec((1, 1, d), lambda b, t: (0, 0, 0)),
                              pl.BlockSpec((d, ne), lambda b, t: (0, 0)),
                              pl.BlockSpec((1, ne), lambda b, t: (0, 0))],
        out_specs=[act, act, pl.BlockSpec((lt, bb, ne), lambda b, t: (t, b, 0))],
        out_shape=(jax.ShapeDtypeStruct((seq, batch, d), F32),
                   jax.ShapeDtypeStruct((seq, batch, d), F32),
                   jax.ShapeDtypeStruct((seq, batch, ne), F32)),
        compiler_params=_cp(("arbitrary", "arbitrary")),
    )(x, y_s5, y_lru, y_rw, z, z, z, mod, mod, mod, w_glu, b_glu.reshape(1, d), wb1, wb2, wb3, w_out,
      norm2_g.reshape(1, 1, d), wr_p, br_p)


ROUTE_IDX, ROUTE_RANK, ROUTE_WT = 0, TOP_K, 2 * TOP_K


def _route_kernel(lg_ref, rec_ref, cnt_ref, run_s):
    tt, nl = lg_ref.shape

    @pl.when(pl.program_id(0) == 0)
    def _():
        run_s[...] = jnp.zeros_like(run_s)

    lane = lax.broadcasted_iota(jnp.int32, (tt, nl), 1).astype(F32)
    neg = jnp.float32(-jnp.inf)
    vals = jnp.where(lane < N_EXPERTS, lg_ref[...], neg)
    tops, hots = [], []
    for _ in range(TOP_K):
        m = jnp.max(vals, axis=-1, keepdims=True)
        idx = jnp.min(jnp.where(vals == m, lane, float(nl)), axis=-1, keepdims=True)
        hot = lane == idx
        vals = jnp.where(hot, neg, vals)
        tops.append((m, idx))
        hots.append(hot)
    es = [jnp.exp(m - tops[0][0]) for m, _ in tops]
    den = es[0]
    for e in es[1:]:
        den = den + e

    mask = jnp.zeros((tt, nl), F32)
    for hot in hots:
        mask = jnp.where(hot, 1.0, mask)
    row = lax.broadcasted_iota(jnp.int32, (tt, tt), 0)
    col = lax.broadcasted_iota(jnp.int32, (tt, tt), 1)
    tri = jnp.where(col < row, 1.0, 0.0).astype(BF16)
    prefix = jnp.dot(tri, mask.astype(BF16), preferred_element_type=F32) + run_s[...]
    run_s[...] = run_s[...] + jnp.sum(mask, axis=0, keepdims=True)
    cnt_ref[...] = run_s[...]

    rec = jnp.zeros((tt, nl), F32)
    for k in range(TOP_K):
        rank = jnp.sum(jnp.where(hots[k], prefix, 0.0), axis=-1, keepdims=True)
        rec = jnp.where(lane == ROUTE_IDX + k, tops[k][1].astype(F32), rec)
        rec = jnp.where(lane == ROUTE_RANK + k, rank, rec)
        rec = jnp.where(lane == ROUTE_WT + k, es[k] / den, rec)
    rec_ref[...] = rec


def _route(logits):
    t, nl = logits.shape
    tt = min(ROUTE_TILE, t)
    while t % tt:
        tt -= SUBLANES
    return pl.pallas_call(
        _route_kernel, name="moe_route",
        grid=(t // tt,),
        in_specs=[pl.BlockSpec((tt, nl), lambda i: (i, 0))],
        out_specs=[pl.BlockSpec((tt, nl), lambda i: (i, 0)), pl.BlockSpec((1, nl), lambda i: (0, 0))],
        out_shape=(jax.ShapeDtypeStruct((t, nl), F32), jax.ShapeDtypeStruct((1, nl), F32)),
        scratch_shapes=[pltpu.VMEM((1, nl), F32)],
        compiler_params=_cp(("arbitrary",)),
    )(logits)


def _expert_kernel(te_ref, tv_ref, xs_ref, wgu_ref, bgu_ref, wd_ref, bd_ref, o_ref, wgu_s, wd_s):
    i = pl.program_id(0)
    de = wd_s.shape[0]
    changed = jnp.logical_or(i == 0, te_ref[i] != te_ref[jnp.maximum(i - 1, 0)])

    @pl.when(changed)
    def _():
        wgu_s[...] = wgu_ref[0].astype(BF16)
        wd_s[...] = wd_ref[0].astype(BF16)

    @pl.when(tv_ref[i] > 0)
    def _():
        gu = jnp.dot(xs_ref[...].astype(BF16), wgu_s[...], preferred_element_type=F32) + bgu_ref[0]
        glu = jnp.minimum(gu[:, :de], SWIGLU_LIMIT)
        lin = jnp.clip(gu[:, de:], -SWIGLU_LIMIT, SWIGLU_LIMIT)
        act = glu * _sigmoid(SWIGLU_ALPHA * glu) * (lin + 1.0)
        o_ref[...] = jnp.dot(act.astype(BF16), wd_s[...], preferred_element_type=F32) + bd_ref[0]

    @pl.when(tv_ref[i] == 0)
    def _():
        o_ref[...] = jnp.zeros_like(o_ref)


def _experts(tile_expert, tile_valid, xs, w_gu, b_gu, w_down, b_down):
    rows, d = xs.shape
    ne, _, n2 = w_gu.shape
    de = w_down.shape[1]
    tm = EXPERT_TILE
    grid_spec = pltpu.PrefetchScalarGridSpec(
        num_scalar_prefetch=2,
        grid=(rows // tm,),
        in_specs=[pl.BlockSpec((tm, d), lambda i, te, tv: (i, 0)),
                  pl.BlockSpec((1, d, n2), lambda i, te, tv: (te[i], 0, 0)),
                  pl.BlockSpec((1, 1, n2), lambda i, te, tv: (te[i], 0, 0)),
                  pl.BlockSpec((1, de, d), lambda i, te, tv: (te[i], 0, 0)),
                  pl.BlockSpec((1, 1, d), lambda i, te, tv: (te[i], 0, 0))],
        out_specs=pl.BlockSpec((tm, d), lambda i, te, tv: (i, 0)),
        scratch_shapes=[pltpu.VMEM((d, n2), BF16), pltpu.VMEM((de, d), BF16)])
    return pl.pallas_call(
        _expert_kernel, name="moe_experts",
        grid_spec=grid_spec,
        out_shape=jax.ShapeDtypeStruct((rows, d), F32),
        compiler_params=_cp(("arbitrary",)),
    )(tile_expert, tile_valid, xs, w_gu, b_gu.reshape(ne, 1, n2), w_down, b_down.reshape(ne, 1, d))


def _combine_kernel(x_ref, og_ref, rec_ref, gt_ref, fg_ref, x2_ref, *, final):
    d = x_ref.shape[-1]
    rec = rec_ref[...]
    y = rec[:, :, ROUTE_WT:ROUTE_WT + 1] * og_ref[:, :, 0:d]
    for k in range(1, TOP_K):
        y = y + rec[:, :, ROUTE_WT + k:ROUTE_WT + k + 1] * og_ref[:, :, k * d:(k + 1) * d]
    x2 = x_ref[...] + gt_ref[...] * y
    if final:
        x2 = x2 * lax.rsqrt(jnp.mean(x2 * x2, axis=-1, keepdims=True) + RMS_EPS) * fg_ref[...]
    x2_ref[...] = x2


def _combine(x1, og_all, tok_off, rec, mod, final_g, final, tokens=256):
    seq, batch, d = x1.shape
    lt, bb = _tiles(seq, batch, tokens)
    nl = rec.shape[-1]
    t_all = og_all.shape[0]
    if t_all % batch == 0 and tok_off % (batch * lt) == 0:
        og = og_all.reshape(t_all // batch, batch, TOP_K * d)
        t0 = tok_off // (batch * lt)
    else:
        og = og_all[tok_off:tok_off + seq * batch].reshape(seq, batch, TOP_K * d)
        t0 = 0
    return pl.pallas_call(
        functools.partial(_combine_kernel, final=final), name="moe_combine",
        grid=(batch // bb, seq // lt),
        in_specs=[pl.BlockSpec((lt, bb, d), lambda b, t: (t, b, 0)),
                  pl.BlockSpec((lt, bb, TOP_K * d), lambda b, t: (t0 + t, b, 0)),
                  pl.BlockSpec((lt, bb, nl), lambda b, t: (t, b, 0)),
                  pl.BlockSpec((1, bb, d), lambda b, t: (0, b, 5)),
                  pl.BlockSpec((1, 1, d), lambda b, t: (0, 0, 0))],
        out_specs=pl.BlockSpec((lt, bb, d), lambda b, t: (t, b, 0)),
        out_shape=jax.ShapeDtypeStruct((seq, batch, d), F32),
        compiler_params=_cp(("arbitrary", "arbitrary")),
    )(x1, og, rec, mod, final_g.reshape(1, 1, d))


def _pad_in_cols(a, axis):
    d = D_MODEL
    main = lax.slice_in_dim(a, 0, 6 * d, axis=axis)
    lora = lax.slice_in_dim(a, 6 * d, 6 * d + RW_LORA, axis=axis)
    gates = lax.slice_in_dim(a, 6 * d + RW_LORA, 9 * d + RW_LORA, axis=axis)
    pad_shape = list(a.shape)
    pad_shape[axis] = LORA_PAD - RW_LORA
    return jnp.concatenate([main, gates, lora, jnp.zeros(pad_shape, a.dtype)], axis=axis)


def _mix_group(x, mod, st, lp, s5p):
    seq, batch, d = x.shape
    s5_re, s5_im, lru_h, lru_conv, rw_s, rw_shift = st
    z = _in_proj(x, mod, lp["norm1_g"], lp["w_in_p"])

    y_s5, n_s5_re, n_s5_im = _s5_mixer(
        z, s5p["ab_re"], s5p["ab_im"], s5p["bb_re"], s5p["bb_im"], s5p["ct_re"], s5p["ct_im"], s5p["d"],
        s5_re.reshape(batch, S5_LANES), s5_im.reshape(batch, S5_LANES))

    y_lru, n_lru_h, tail = _lru_mixer(
        z, lru_conv.transpose(1, 0, 2), lp["lru_conv_w"], lp["lru_conv_b"], lp["lru_w_a"], lp["lru_b_a"],
        lp["lru_w_x"], lp["lru_b_x"], lp["lru_lam"], lru_h)

    y_rw, n_rw_s = _rwkv(z, rw_shift, rw_s, lp)
    n_rw_shift = jnp.concatenate([z[seq - 1, :, COL_R:COL_R + 3 * d], z[seq - 1, :, COL_LORA:COL_LORA + RW_LORA]],
                                 axis=-1)

    x1, h2, logits = _merge(x, y_s5, y_lru, y_rw, z, mod, lp["s5_w_glu"], lp["s5_b_glu"], lp["w_br_s5"],
                            lp["w_br_lru"], lp["w_br_rw"], lp["w_out"], lp["norm2_g"], lp["wr_p"], lp["br_p"])
    new = (n_s5_re.reshape(batch, S5_GROUPS, S5_STATE), n_s5_im.reshape(batch, S5_GROUPS, S5_STATE),
           n_lru_h, tail.transpose(1, 0, 2), n_rw_s, n_rw_shift)
    return x1, h2, logits, new


def _moe(h2_all, logits_all, lp):
    t, d = h2_all.shape
    tm = EXPERT_TILE
    rec, cnt = _route(logits_all)
    idx = rec[:, ROUTE_IDX:ROUTE_IDX + TOP_K].astype(jnp.int32)
    rank = rec[:, ROUTE_RANK:ROUTE_RANK + TOP_K].astype(jnp.int32)
    counts = cnt[0, :N_EXPERTS].astype(jnp.int32)
    padded = ((counts + tm - 1) // tm) * tm
    ends = jnp.cumsum(padded)
    offs = ends - padded
    pos = offs[idx] + rank
    n_tiles = (t * TOP_K + N_EXPERTS * (tm - 1) + tm - 1) // tm
    rows = n_tiles * tm
    src = jnp.zeros((rows,), jnp.int32).at[pos.reshape(-1)].set(
        jnp.repeat(jnp.arange(t, dtype=jnp.int32), TOP_K))
    starts = jnp.arange(n_tiles, dtype=jnp.int32) * tm
    tile_valid = (starts < ends[-1]).astype(jnp.int32)
    owner = lambda row: jnp.sum((ends[None, :] <= row[:, None]).astype(jnp.int32), axis=1)
    last = owner(ends[-1:] - 1)[0]
    tile_expert = jnp.clip(jnp.where(tile_valid > 0, owner(starts), last), 0, N_EXPERTS - 1)
    xs = h2_all.at[src].get(mode="promise_in_bounds")
    os_ = _experts(tile_expert, tile_valid, xs, lp["moe_w_gu"], lp["moe_b_gu"], lp["moe_w_down"], lp["moe_b_down"])
    og = os_.at[pos.reshape(-1)].get(mode="promise_in_bounds").reshape(t, TOP_K * d)
    return og, rec


def kernel(x_prompt, x_sample, state_s5_re, state_s5_im, state_lru_h, cache_lru_conv, state_rwkv, cache_rwkv_shift, c_prompt, c_sample, w_mod, b_mod, norm1_g, w_in, s5_lam_re, s5_lam_im, s5_log_step, s5_b_re, s5_b_im, s5_c_re, s5_c_im, s5_d, s5_w_glu, s5_b_glu, lru_conv_w, lru_conv_b, lru_w_a, lru_b_a, lru_w_x, lru_b_x, lru_lam, rw_mu, rw_w0, rw_w2, rw_a0, rw_a2, rw_g2, rw_k_k, rw_k_a, rw_r_k, rw_ln_w, rw_ln_b, w_br_s5, w_br_lru, w_br_rw, w_out, norm2_g, moe_w_router, moe_b_router, moe_w_gu, moe_b_gu, moe_w_down, moe_b_down, final_g):
    depth = w_mod.shape[0]
    d = D_MODEL
    bp, lp_len = x_prompt.shape[0], x_prompt.shape[1]
    bs, ls_len = x_sample.shape[0], x_sample.shape[1]

    mod_all = _modulation(jnp.concatenate([c_prompt, c_sample], axis=0), w_mod, b_mod)

    ab_re, ab_im, q_re, q_im = _s5_params(s5_lam_re, s5_lam_im, s5_log_step)
    shp = (depth, S5_GROUPS, S5_STATE)
    q_re, q_im = q_re.reshape(shp)[..., None], q_im.reshape(shp)[..., None]
    bbar_re = q_re * s5_b_re - q_im * s5_b_im
    bbar_im = q_re * s5_b_im + q_im * s5_b_re
    per = LANES // S5_GROUP

    xs = [x_prompt.transpose(1, 0, 2), x_sample.transpose(1, 0, 2)]
    zeros_like_state = lambda s, b: jnp.zeros((b,) + s.shape[2:], F32)
    sample_states = (state_s5_re, state_s5_im, state_lru_h, cache_lru_conv, state_rwkv, cache_rwkv_shift)
    collected = [tuple([] for _ in sample_states), tuple([] for _ in sample_states)]

    for l in range(depth):
        lp = dict(
            norm1_g=norm1_g[l], w_in_p=_pad_in_cols(w_in[l], 1).astype(BF16),
            lru_conv_w=lru_conv_w[l], lru_conv_b=lru_conv_b[l], lru_w_a=lru_w_a[l], lru_b_a=lru_b_a[l],
            lru_w_x=lru_w_x[l], lru_b_x=lru_b_x[l], lru_lam=lru_lam[l],
            rw_mu=rw_mu[l], rw_w0=rw_w0[l], rw_a0=rw_a0[l],
            rw_w2p=jnp.concatenate([rw_w2[l], jnp.zeros((LANES - RW_W_LORA, d), F32)], axis=0).astype(BF16),
            rw_a2p=jnp.concatenate([jnp.zeros((RW_W_LORA, d), F32), rw_a2[l]], axis=0).astype(BF16),
            rw_g2p=jnp.concatenate([rw_g2[l], jnp.zeros((2 * LANES - RW_G_LORA, d), F32)], axis=0).astype(BF16),
            rw_k_k=rw_k_k[l], rw_k_a=rw_k_a[l], rw_r_k=rw_r_k[l], rw_ln_w=rw_ln_w[l], rw_ln_b=rw_ln_b[l],
            s5_w_glu=s5_w_glu[l].astype(BF16), s5_b_glu=s5_b_glu[l], w_br_s5=w_br_s5[l].astype(BF16),
            w_br_lru=w_br_lru[l].astype(BF16), w_br_rw=w_br_rw[l].astype(BF16), w_out=w_out[l].astype(BF16),
            norm2_g=norm2_g[l],
            wr_p=jnp.concatenate([moe_w_router[l], jnp.zeros((d, LANES - N_EXPERTS), F32)], axis=1),
            br_p=jnp.concatenate([moe_b_router[l], jnp.zeros((LANES - N_EXPERTS,), F32)]).reshape(1, LANES),
            moe_w_gu=moe_w_gu[l], moe_b_gu=moe_b_gu[l], moe_w_down=moe_w_down[l], moe_b_down=moe_b_down[l])
        g0 = l * S5_GROUPS
        s5p = dict(
            ab_re=ab_re[g0:g0 + S5_GROUPS].reshape(1, S5_LANES), ab_im=ab_im[g0:g0 + S5_GROUPS].reshape(1, S5_LANES),
            bb_re=_block_diag(bbar_re[l].transpose(0, 2, 1), per).astype(BF16),
            bb_im=_block_diag(bbar_im[l].transpose(0, 2, 1), per).astype(BF16),
            ct_re=_block_diag(s5_c_re[l].transpose(0, 2, 1), per).astype(BF16),
            ct_im=_block_diag(s5_c_im[l].transpose(0, 2, 1), per).astype(BF16),
            d=s5_d[l].reshape(1, d))

        mods = [mod_all[l, :bp][None], mod_all[l, bp:][None]]
        states = [tuple(zeros_like_state(s, bp) for s in sample_states), tuple(s[l] for s in sample_states)]
        x1s, h2s, lgs = [], [], []
        for gi in range(2):
            x1, h2, lg, new = _mix_group(xs[gi], mods[gi], states[gi], lp, s5p)
            x1s.append(x1)
            h2s.append(h2.reshape(-1, d))
            lgs.append(lg.reshape(-1, LANES))
            for lst, s in zip(collected[gi], new):
                lst.append(s)

        og, rec = _moe(jnp.concatenate(h2s[::-1], axis=0), jnp.concatenate(lgs[::-1], axis=0), lp)
        ts = ls_len * bs
        offs = [ts, 0]
        recs = [rec[ts:].reshape(lp_len, bp, LANES), rec[:ts].reshape(ls_len, bs, LANES)]
        xs = [_combine(x1s[gi], og, offs[gi], recs[gi], mods[gi], final_g, final=(l == depth - 1))
              for gi in range(2)]

    y_prompt = xs[0].transpose(1, 0, 2)
    y_sample = xs[1].transpose(1, 0, 2)
    p_states = tuple(jnp.stack(lst) for lst in collected[0])
    s_states = tuple(jnp.stack(lst) for lst in collected[1])
    return (y_prompt, y_sample) + p_states + s_states
```

```python
import functools
import math

import jax
import jax.numpy as jnp
from jax import lax
from jax.experimental import pallas as pl
from jax.experimental.pallas import tpu as pltpu

F32 = jnp.float32
BF16 = jnp.bfloat16
HIGHEST = lax.Precision.HIGHEST

D_MODEL = 1024
RMS_EPS = 1e-5
S5_GROUP = 16
S5_GROUPS = D_MODEL // S5_GROUP
S5_STATE = 64
S5_LANES = S5_GROUPS * S5_STATE
LRU_HEADS = 8
LRU_BLOCK = D_MODEL // LRU_HEADS
CONV_WIDTH = 4
LRU_C = 8.0
RW_HEAD = 64
RW_HEADS = D_MODEL // RW_HEAD
RW_W_LORA = 64
RW_A_LORA = 64
RW_G_LORA = 160
RW_LORA = RW_W_LORA + RW_A_LORA + RW_G_LORA
RW_COLS = 3 * D_MODEL + RW_LORA
RW_LN_EPS = 64e-5
N_EXPERTS = 32
TOP_K = 4
SWIGLU_ALPHA = 1.702
SWIGLU_LIMIT = 7.0

LANES = 128
SUBLANES = 8
VMEM_LIMIT = 56 * 1024 * 1024

LORA_PAD = 512
COL_S5 = 0
COL_XLRU = 1 * D_MODEL
COL_GLRU = 2 * D_MODEL
COL_R = 3 * D_MODEL
COL_K = 4 * D_MODEL
COL_V = 5 * D_MODEL
COL_GATE = 6 * D_MODEL
COL_LORA = 9 * D_MODEL
D_IN_PAD = COL_LORA + LORA_PAD

S5_LANE_BLOCK = LANES
S5_STATE_BLOCK = (LANES // S5_GROUP) * S5_STATE
EXPERT_TILE = 512
ROUTE_TILE = 512


def _cp(sem, vmem=VMEM_LIMIT):
    return pltpu.CompilerParams(dimension_semantics=sem, vmem_limit_bytes=vmem)


def _gelu(x):
    return 0.5 * x * (1.0 + jnp.tanh(math.sqrt(2.0 / math.pi) * (x + 0.044715 * (x * x * x))))


def _sigmoid(x):
    return 1.0 / (1.0 + jnp.exp(-x))


def _softplus(x):
    return jnp.maximum(x, 0.0) + jnp.log1p(jnp.exp(-jnp.abs(x)))


def _tiles(seq, batch, tokens):
    bb = min(batch, max(SUBLANES, (tokens // seq) // SUBLANES * SUBLANES))
    while batch % bb:
        bb -= SUBLANES
    lt = max(1, min(seq, tokens // bb))
    while seq % lt:
        lt -= 1
    return lt, bb


def _swap_kernel(x_ref, o_ref):
    n0, n1, _ = x_ref.shape
    if n0 <= n1:
        for i in range(n0):
            o_ref[:, i, :] = x_ref[i]
    else:
        for i in range(n1):
            o_ref[i] = x_ref[:, i, :]


def _swap_leading(x, tokens=1024):
    a, b, d = x.shape
    ta = min(a, SUBLANES) if a <= b else min(a, max(SUBLANES, tokens // b))
    tb = min(b, max(SUBLANES, tokens // ta))
    while a % ta:
        ta -= SUBLANES
    while b % tb:
        tb -= SUBLANES
    return pl.pallas_call(
        _swap_kernel, name="swap_leading",
        grid=(a // ta, b // tb),
        in_specs=[pl.BlockSpec((ta, tb, d), lambda i, j: (i, j, 0))],
        out_specs=pl.BlockSpec((tb, ta, d), lambda i, j: (j, i, 0)),
        out_shape=jax.ShapeDtypeStruct((b, a, d), x.dtype),
        compiler_params=_cp(("arbitrary", "arbitrary")),
    )(x)


def _mod_kernel(c_ref, w_ref, b_ref, o_ref):
    c = c_ref[...]
    s = c * _sigmoid(c)
    o_ref[0] = jnp.dot(s, w_ref[0], precision=HIGHEST, preferred_element_type=F32) + b_ref[0]


def _modulation(c, w_mod, b_mod):
    depth, d, n = w_mod.shape
    bc = c.shape[0]
    tn = 1536
    return pl.pallas_call(
        _mod_kernel, name="adaln_mod",
        grid=(depth, n // tn),
        in_specs=[pl.BlockSpec((bc, d), lambda l, j: (0, 0)),
                  pl.BlockSpec((1, d, tn), lambda l, j: (l, 0, j)),
                  pl.BlockSpec((1, 1, tn), lambda l, j: (l, 0, j))],
        out_specs=pl.BlockSpec((1, bc, tn), lambda l, j: (l, 0, j)),
        out_shape=jax.ShapeDtypeStruct((depth, bc, n), F32),
        compiler_params=_cp(("arbitrary", "arbitrary")),
    )(c, w_mod, b_mod.reshape(depth, 1, n))


def _in_kernel(x_ref, sh_ref, sc_ref, g_ref, w_ref, o_ref, h_scr):
    lt, bb, d = x_ref.shape

    @pl.when(pl.program_id(2) == 0)
    def _():
        x = x_ref[...]
        y = x * lax.rsqrt(jnp.mean(x * x, axis=-1, keepdims=True) + RMS_EPS) * g_ref[...]
        h = y * (1.0 + sc_ref[...]) + sh_ref[...]
        h_scr[...] = h.reshape(lt * bb, d).astype(BF16)

    o = jnp.dot(h_scr[...], w_ref[...], preferred_element_type=F32)
    o_ref[...] = o.reshape(lt, bb, o.shape[-1])


def _in_proj(x, mod, norm_g, w_in_p, tokens=1024, tn=512):
    seq, batch, d = x.shape
    lt, bb = _tiles(seq, batch, tokens)
    n = w_in_p.shape[1]
    return pl.pallas_call(
        _in_kernel, name="in_proj",
        grid=(batch // bb, seq // lt, n // tn),
        in_specs=[pl.BlockSpec((lt, bb, d), lambda b, t, j: (t, b, 0)),
                  pl.BlockSpec((1, bb, d), lambda b, t, j: (0, b, 0)),
                  pl.BlockSpec((1, bb, d), lambda b, t, j: (0, b, 1)),
                  pl.BlockSpec((1, 1, d), lambda b, t, j: (0, 0, 0)),
                  pl.BlockSpec((d, tn), lambda b, t, j: (0, j))],
        out_specs=pl.BlockSpec((lt, bb, tn), lambda b, t, j: (t, b, j)),
        out_shape=jax.ShapeDtypeStruct((seq, batch, n), F32),
        scratch_shapes=[pltpu.VMEM((lt * bb, d), BF16)],
        compiler_params=_cp(("arbitrary", "arbitrary", "arbitrary")),
    )(x, mod, mod, norm_g.reshape(1, 1, d), w_in_p)


def _s5_kernel(u_ref, bbr_ref, bbi_ref, ar_ref, ai_ref, ctr_ref, cti_ref, d_ref, h0r_ref, h0i_ref,
               y_ref, hr_ref, hi_ref, sr, si):
    lt, bb, nl = u_ref.shape
    ns = sr.shape[-1]

    @pl.when(pl.program_id(2) == 0)
    def _():
        hr_ref[...] = h0r_ref[...]
        hi_ref[...] = h0i_ref[...]

    u2 = u_ref[...].reshape(lt * bb, nl)
    ub = u2.astype(BF16)
    sr[...] = jnp.dot(ub, bbr_ref[0], preferred_element_type=F32).reshape(lt, bb, ns)
    si[...] = jnp.dot(ub, bbi_ref[0], preferred_element_type=F32).reshape(lt, bb, ns)
    ar = jnp.broadcast_to(ar_ref[...], (bb, ns))
    ai = jnp.broadcast_to(ai_ref[...], (bb, ns))

    def body(i, carry):
        hr, hi = carry
        nr = ar * hr - ai * hi + sr[i]
        ni = ar * hi + ai * hr + si[i]
        sr[i] = nr
        si[i] = ni
        return nr, ni

    hr, hi = lax.fori_loop(0, lt, body, (hr_ref[...], hi_ref[...]), unroll=min(lt, 8))
    hr_ref[...] = hr
    hi_ref[...] = hi
    y = (jnp.dot(sr[...].reshape(lt * bb, ns).astype(BF16), ctr_ref[0], preferred_element_type=F32)
         - jnp.dot(si[...].reshape(lt * bb, ns).astype(BF16), cti_ref[0], preferred_element_type=F32)
         + d_ref[...] * u2)
    y_ref[...] = y.reshape(lt, bb, nl)


def _s5_params_kernel(lr_ref, li_ref, ls_ref, abr_ref, abi_ref, qr_ref, qi_ref):
    lr = lr_ref[...]
    li = li_ref[...]
    step = jnp.exp(ls_ref[...])
    mag = jnp.exp(lr * step)
    ab_re = mag * jnp.cos(li * step)
    ab_im = mag * jnp.sin(li * step)
    den = lr * lr + li * li
    abr_ref[...] = ab_re
    abi_ref[...] = ab_im
    qr_ref[...] = ((ab_re - 1.0) * lr + ab_im * li) / den
    qi_ref[...] = (ab_im * lr - (ab_re - 1.0) * li) / den


def _s5_params(lam_re, lam_im, log_step):
    depth, g, n = lam_re.shape
    shp = jax.ShapeDtypeStruct((depth * g, n), F32)
    return pl.pallas_call(_s5_params_kernel, name="s5_params", out_shape=(shp, shp, shp, shp))(
        lam_re.reshape(depth * g, n), lam_im.reshape(depth * g, n), log_step.reshape(depth * g, 1))


def _block_diag(blocks, per):
    g, a, b = blocks.shape
    x = blocks.reshape(g // per, per, a, b)
    eye = jnp.eye(per, dtype=blocks.dtype)
    return jnp.einsum("gpab,pq->gpaqb", x, eye).reshape(g // per, per * a, per * b)


def _s5_mixer(z, ab_re, ab_im, bb_re, bb_im, ct_re, ct_im, d_skip, h0_re, h0_im, tokens=512):
    seq, batch, _ = z.shape
    lt, bb = _tiles(seq, batch, tokens)
    nl, ns = S5_LANE_BLOCK, S5_STATE_BLOCK
    nblk = D_MODEL // nl
    col0 = COL_S5 // nl
    y, hr, hi = pl.pallas_call(
        _s5_kernel, name="s5_mixer",
        grid=(batch // bb, nblk, seq // lt),
        in_specs=[pl.BlockSpec((lt, bb, nl), lambda b, j, t: (t, b, col0 + j)),
                  pl.BlockSpec((1, nl, ns), lambda b, j, t: (j, 0, 0)),
                  pl.BlockSpec((1, nl, ns), lambda b, j, t: (j, 0, 0)),
                  pl.BlockSpec((1, ns), lambda b, j, t: (0, j)),
                  pl.BlockSpec((1, ns), lambda b, j, t: (0, j)),
                  pl.BlockSpec((1, ns, nl), lambda b, j, t: (j, 0, 0)),
                  pl.BlockSpec((1, ns, nl), lambda b, j, t: (j, 0, 0)),
                  pl.BlockSpec((1, nl), lambda b, j, t: (0, j)),
                  pl.BlockSpec((bb, ns), lambda b, j, t: (b, j)),
                  pl.BlockSpec((bb, ns), lambda b, j, t: (b, j))],
        out_specs=[pl.BlockSpec((lt, bb, nl), lambda b, j, t: (t, b, j)),
                   pl.BlockSpec((bb, ns), lambda b, j, t: (b, j)),
                   pl.BlockSpec((bb, ns), lambda b, j, t: (b, j))],
        out_shape=(jax.ShapeDtypeStruct((seq, batch, D_MODEL), F32),
                   jax.ShapeDtypeStruct((batch, S5_LANES), F32),
                   jax.ShapeDtypeStruct((batch, S5_LANES), F32)),
        scratch_shapes=[pltpu.VMEM((lt, bb, ns), F32), pltpu.VMEM((lt, bb, ns), F32)],
        compiler_params=_cp(("arbitrary", "arbitrary", "arbitrary")),
    )(z, bb_re, bb_im, ab_re, ab_im, ct_re, ct_im, d_skip, h0_re, h0_im)
    return y, hr, hi


def _lru_kernel(x_ref, g_ref, cbuf_ref, cw_ref, cb_ref, wa_ref, ba_ref, wx_ref, bx_ref, lam_ref, h0_ref,
                y_ref, hn_ref, tail_ref, prev_s, a_s, h_s):
    lt, bb, w = x_ref.shape
    taps = cw_ref.shape[0]

    @pl.when(pl.program_id(2) == 0)
    def _():
        hn_ref[...] = h0_ref[...]
        prev_s[...] = cbuf_ref[...]

    xp = jnp.concatenate([prev_s[...], x_ref[...]], axis=0)
    xc = cb_ref[...] + xp[0:lt] * cw_ref[0]
    for tap in range(1, taps):
        xc = xc + xp[tap:tap + lt] * cw_ref[tap]
    prev_s[...] = xp[lt:lt + taps - 1]
    tail_ref[...] = xp[lt:lt + taps - 1]

    xc2 = xc.reshape(lt * bb, w)
    xb = xc2.astype(BF16)
    r = _sigmoid(jnp.dot(xb, wa_ref[0], preferred_element_type=F32) + ba_ref[...])
    i = _sigmoid(jnp.dot(xb, wx_ref[0], preferred_element_type=F32) + bx_ref[...])
    log_a = (-LRU_C) * r * _softplus(-lam_ref[...])
    a = jnp.exp(log_a)
    th = jnp.tanh(log_a)
    neg_expm1 = -2.0 * th / (1.0 - th)
    b = jnp.sqrt(neg_expm1) * (i * xc2)
    a_s[...] = a.reshape(lt, bb, w)
    h_s[...] = b.reshape(lt, bb, w)

    def body(t, h):
        h = a_s[t] * h + h_s[t]
        h_s[t] = h
        return h

    hn_ref[...] = lax.fori_loop(0, lt, body, hn_ref[...], unroll=min(lt, 8))
    y_ref[...] = h_s[...] * _gelu(g_ref[...])


def _lru_mixer(z, conv_buf_t, conv_w, conv_b, w_a, b_a, w_x, b_x, lam, h0, tokens=1024):
    seq, batch, _ = z.shape
    lt, bb = _tiles(seq, batch, tokens)
    w = LRU_BLOCK
    xcol, gcol = COL_XLRU // w, COL_GLRU // w
    taps = CONV_WIDTH
    vec = lambda a: a.reshape(1, D_MODEL)
    vspec = pl.BlockSpec((1, w), lambda b, h, t: (0, h))
    y, hn, tail = pl.pallas_call(
        _lru_kernel, name="rglru_mixer",
        grid=(batch // bb, LRU_HEADS, seq // lt),
        in_specs=[pl.BlockSpec((lt, bb, w), lambda b, h, t: (t, b, xcol + h)),
                  pl.BlockSpec((lt, bb, w), lambda b, h, t: (t, b, gcol + h)),
                  pl.BlockSpec((taps - 1, bb, w), lambda b, h, t: (0, b, h)),
                  pl.BlockSpec((taps, 1, w), lambda b, h, t: (0, 0, h)),
                  vspec,
                  pl.BlockSpec((1, w, w), lambda b, h, t: (h, 0, 0)),
                  vspec,
                  pl.BlockSpec((1, w, w), lambda b, h, t: (h, 0, 0)),
                  vspec, vspec,
                  pl.BlockSpec((bb, w), lambda b, h, t: (b, h))],
        out_specs=[pl.BlockSpec((lt, bb, w), lambda b, h, t: (t, b, h)),
                   pl.BlockSpec((bb, w), lambda b, h, t: (b, h)),
                   pl.BlockSpec((taps - 1, bb, w), lambda b, h, t: (0, b, h))],
        out_shape=(jax.ShapeDtypeStruct((seq, batch, D_MODEL), F32),
                   jax.ShapeDtypeStruct((batch, D_MODEL), F32),
                   jax.ShapeDtypeStruct((taps - 1, batch, D_MODEL), F32)),
        scratch_shapes=[pltpu.VMEM((taps - 1, bb, w), F32), pltpu.VMEM((lt, bb, w), F32),
                        pltpu.VMEM((lt, bb, w), F32)],
        compiler_params=_cp(("arbitrary", "arbitrary", "arbitrary")),
    )(z, z, conv_buf_t, conv_w.reshape(taps, 1, D_MODEL), vec(conv_b), w_a.astype(BF16), vec(b_a),
      w_x.astype(BF16), vec(b_x), vec(lam), h0)
    return y, hn, tail


def _rwkv_kernel(zr_ref, zk_ref, zv_ref, zl_ref, sh_ref, shl_ref, mu_ref, mul_ref, w0_ref, w2_ref, a0_ref, a2_ref,
                 g2_ref, kk_ref, ka_ref, rk_ref, lnw_ref, lnb_ref, s0_ref,
                 y_ref, sn_ref,
                 prev_s, prevl_s, s_scr, r_s, w_s, k_s, v_s, nkk_s, b_s, y_s):
    lt, bb, wd = zr_ref.shape
    kp = wd // LANES
    hd = RW_HEAD
    nl = zl_ref.shape[-1]
    n = lt * bb
    ti = pl.program_id(2)

    @pl.when(ti == 0)
    def _():
        prev_s[...] = sh_ref[...]
        prevl_s[...] = shl_ref[...]
        for i in range(hd):
            rows = [s0_ref[:, 2 * c + par, i, :] for par in range(2) for c in range(kp)]
            s_scr[:, i, :] = jnp.concatenate(rows, axis=0).T

    def shifted(z_ref, prev, mu):
        z = z_ref[...]
        zp = jnp.concatenate([prev, z[0:lt - 1]], axis=0) if lt > 1 else prev
        return z + (zp - z) * mu, z[lt - 1:lt]

    r, last_r = shifted(zr_ref, prev_s[0:1], mu_ref[0:1])
    k, last_k = shifted(zk_ref, prev_s[1:2], mu_ref[1:2])
    v, last_v = shifted(zv_ref, prev_s[2:3], mu_ref[2:3])
    lo, last_l = shifted(zl_ref, prevl_s[...], mul_ref[...])
    prev_s[0:1] = last_r
    prev_s[1:2] = last_k
    prev_s[2:3] = last_v
    prevl_s[...] = last_l

    lo2 = lo.reshape(n, nl)
    wa_in = lo2[:, 0:LANES]
    g_in = lo2[:, LANES:3 * LANES]
    wpre = w0_ref[...] + jnp.dot(jnp.tanh(wa_in).astype(BF16), w2_ref[...], preferred_element_type=F32)
    decay = jnp.exp(-jnp.exp(-_softplus(-wpre) - 0.5))
    a = _sigmoid(a0_ref[...] + jnp.dot(wa_in.astype(BF16), a2_ref[...], preferred_element_type=F32))
    g = jnp.dot(_sigmoid(g_in).astype(BF16), g2_ref[...], preferred_element_type=F32)
    k2 = k.reshape(n, wd)
    shp = (lt, bb, wd)

    def to_lanes(q):
        rows = jnp.concatenate([q[:, :, c * LANES:(c + 1) * LANES] for c in range(kp)], axis=1) if kp > 1 else q
        t = jnp.swapaxes(rows, 1, 2)
        return jnp.concatenate([t[:, 0:hd, :], t[:, hd:2 * hd, :]], axis=2)

    r_s[...] = to_lanes(r)
    w_s[...] = to_lanes(decay.reshape(shp))
    k_s[...] = to_lanes((k2 * (1.0 + (a - 1.0) * ka_ref[...])).reshape(shp))
    v_s[...] = to_lanes(v)
    kk = to_lanes((k2 * kk_ref[...]).reshape(shp))
    kkn = kk * lax.rsqrt(jnp.maximum(jnp.sum(kk * kk, axis=1, keepdims=True), 1e-24))
    nkk_s[...] = -kkn
    b_s[...] = kkn * to_lanes(a.reshape(shp))

    def step(t, carry):
        acc = [jnp.zeros((hd, LANES), F32) for _ in range(4)]
        for j in range(hd):
            acc[j % 4] = acc[j % 4] + s_scr[j] * nkk_s[t, pl.ds(j, 1), :]
        sa = (acc[0] + acc[1]) + (acc[2] + acc[3])
        vt = v_s[t]
        yac = [jnp.zeros((hd, LANES), F32) for _ in range(4)]
        for j in range(hd):
            sj = (s_scr[j] * w_s[t, pl.ds(j, 1), :] + sa * b_s[t, pl.ds(j, 1), :]
                  + vt * k_s[t, pl.ds(j, 1), :])
            s_scr[j] = sj
            yac[j % 4] = yac[j % 4] + sj * r_s[t, pl.ds(j, 1), :]
        y_s[t] = (yac[0] + yac[1]) + (yac[2] + yac[3])
        return carry

    lax.fori_loop(0, lt, step, 0)

    ys = y_s[...]
    mean = jnp.mean(ys, axis=1, keepdims=True)
    yc = ys - mean
    var = jnp.mean(yc * yc, axis=1, keepdims=True)
    yn = yc * lax.rsqrt(var + RW_LN_EPS)
    bonus = jnp.sum(r_s[...] * k_s[...] * rk_ref[...], axis=1, keepdims=True)
    o = yn * lnw_ref[...] + lnb_ref[...] + bonus * v_s[...]
    o = jnp.swapaxes(jnp.concatenate([o[:, :, 0:hd], o[:, :, hd:2 * hd]], axis=1), 1, 2)
    nat = jnp.concatenate([o[:, c * bb:(c + 1) * bb, :] for c in range(kp)], axis=2) if kp > 1 else o
    y_ref[...] = nat * g.reshape(shp)

    @pl.when(ti == pl.num_programs(2) - 1)
    def _():
        for i in range(hd):
            m = s_scr[:, i, :].T
            for par in range(2):
                for c in range(kp):
                    r0 = (par * kp + c) * bb
                    sn_ref[:, 2 * c + par, i, :] = m[r0:r0 + bb, :]


def _rwkv(z, rw_shift, rw_s, lp, lt=32):
    seq, batch, _ = z.shape
    d, hd, nl = D_MODEL, RW_HEAD, LORA_PAD
    bb, kp = _rw_tiles(batch)
    wd = LANES * kp
    n_pg = d // wd
    lt = min(lt, seq)
    while seq % lt:
        lt -= 1
    pad = lambda a: jnp.concatenate([a, jnp.zeros(a.shape[:-1] + (nl - RW_LORA,), a.dtype)], axis=-1)
    sh = rw_shift[:, 0:3 * d].reshape(batch, 3, d).transpose(1, 0, 2)
    shl = pad(rw_shift[:, 3 * d:])[None]
    mu = lp["rw_mu"][0:3 * d].reshape(3, 1, d)
    mul = pad(lp["rw_mu"][3 * d:]).reshape(1, 1, nl)
    vec = lambda a: a.reshape(1, d)
    zspec = lambda col: pl.BlockSpec((lt, bb, wd), lambda b, g, t: (t, b, col // wd + g))
    vspec = pl.BlockSpec((1, wd), lambda b, g, t: (0, g))
    wspec = lambda rows: pl.BlockSpec((rows, wd), lambda b, g, t: (0, g))
    pspec = pl.BlockSpec((1, hd, LANES), lambda b, g, t: (g, 0, 0))
    sspec = pl.BlockSpec((bb, 2 * kp, hd, hd), lambda b, g, t: (b, g, 0, 0))
    chunk = pltpu.VMEM((lt, hd, LANES), F32)
    y, sn = pl.pallas_call(
        _rwkv_kernel, name="rwkv7_mixer",
        grid=(batch // bb, n_pg, seq // lt),
        in_specs=[zspec(COL_R), zspec(COL_K), zspec(COL_V),
                  pl.BlockSpec((lt, bb, nl), lambda b, g, t: (t, b, COL_LORA // nl)),
                  pl.BlockSpec((3, bb, wd), lambda b, g, t: (0, b, g)),
                  pl.BlockSpec((1, bb, nl), lambda b, g, t: (0, b, 0)),
                  pl.BlockSpec((3, 1, wd), lambda b, g, t: (0, 0, g)),
                  pl.BlockSpec((1, 1, nl), lambda b, g, t: (0, 0, 0)),
                  vspec, wspec(LANES), vspec, wspec(LANES), wspec(2 * LANES), vspec, vspec,
                  pspec, pspec, pspec, sspec],
        out_specs=[pl.BlockSpec((lt, bb, wd), lambda b, g, t: (t, b, g)), sspec],
        out_shape=(jax.ShapeDtypeStruct((seq, batch, d), F32),
                   jax.ShapeDtypeStruct((batch, RW_HEADS, hd, hd), F32)),
        scratch_shapes=[pltpu.VMEM((3, bb, wd), F32), pltpu.VMEM((1, bb, nl), F32),
                        pltpu.VMEM((hd, hd, LANES), F32)] + [chunk] * 7,
        compiler_params=_cp(("arbitrary", "arbitrary", "arbitrary")),
    )(z, z, z, z, sh, shl, mu, mul, vec(lp["rw_w0"]), lp["rw_w2p"], vec(lp["rw_a0"]), lp["rw_a2p"], lp["rw_g2p"],
      vec(lp["rw_k_k"]), vec(lp["rw_k_a"]), _rw_param_blocks(lp["rw_r_k"].reshape(-1), bb, kp),
      _rw_param_blocks(lp["rw_ln_w"], bb, kp), _rw_param_blocks(lp["rw_ln_b"], bb, kp), rw_s)
    return y, sn


def _rw_tiles(batch):
    bb = min(batch, LANES // 2)
    assert (LANES // 2) % bb == 0 and batch % bb == 0 and bb % SUBLANES == 0, batch
    return bb, (LANES // 2) // bb


def _rw_param_blocks(p, bb, kp):
    n_pg = RW_HEADS // (2 * kp)
    x = p.reshape(n_pg, kp, 2, RW_HEAD).transpose(0, 3, 2, 1)
    return jnp.broadcast_to(x[..., None], (n_pg, RW_HEAD, 2, kp, bb)).reshape(n_pg, RW_HEAD, LANES)


def _merge_kernel(x_ref, ys5_ref, ylru_ref, yrw_ref, zg1_ref, zg2_ref, zg3_ref,
                  gt1_ref, sc2_ref, sh2_ref, wglu_ref, bglu_ref, wb1_ref, wb2_ref, wb3_ref, wout_ref,
                  n2g_ref, wr_ref, br_ref,
                  x1_ref, h2_ref, lg_ref):
    lt, bb, d = x_ref.shape
    n = lt * bb
    mm = lambda a, w_ref: jnp.dot(a.astype(BF16), w_ref[...], preferred_element_type=F32)
    flat = lambda ref: ref[...].reshape(n, d)

    y1 = _gelu(flat(ys5_ref))
    y1 = y1 * _sigmoid(mm(y1, wglu_ref) + bglu_ref[...])
    y3 = flat(yrw_ref)
    merged = (_sigmoid(flat(zg1_ref)) * mm(y1, wb1_ref)
              + _sigmoid(flat(zg2_ref)) * mm(flat(ylru_ref), wb2_ref)
              + _sigmoid(flat(zg3_ref)) * mm(y3, wb3_ref))
    upd = mm(merged, wout_ref).reshape(lt, bb, d)
    x1 = x_ref[...] + gt1_ref[...] * upd
    x1_ref[...] = x1
    y = x1 * lax.rsqrt(jnp.mean(x1 * x1, axis=-1, keepdims=True) + RMS_EPS) * n2g_ref[...]
    h2 = y * (1.0 + sc2_ref[...]) + sh2_ref[...]
    h2_ref[...] = h2
    logits = jnp.dot(h2.reshape(n, d), wr_ref[...], precision=HIGHEST, preferred_element_type=F32)
    lg_ref[...] = (logits + br_ref[...]).reshape(lt, bb, lg_ref.shape[-1])


def _merge(x, y_s5, y_lru, y_rw, z, mod, w_glu, b_glu, wb1, wb2, wb3, w_out, norm2_g, wr_p, br_p,
           tokens=256):
    seq, batch, d = x.shape
    lt, bb = _tiles(seq, batch, tokens)
    act = pl.BlockSpec((lt, bb, d), lambda b, t: (t, b, 0))
    gate = lambda i: pl.BlockSpec((lt, bb, d), lambda b, t: (t, b, COL_GATE // d + i))
    modspec = lambda i: pl.BlockSpec((1, bb, d), lambda b, t: (0, b, i))
    wspec = pl.BlockSpec((d, d), lambda b, t: (0, 0), pipeline_mode=pl.Buffered(1))
    vspec = pl.BlockSpec((1, d), lambda b, t: (0, 0))
    ne = wr_p.shape[1]
    return pl.pallas_call(
        _merge_kernel, name="merge_norm2_router",
        grid=(batch // bb, seq // lt),
        in_specs=[act] * 4 + [gate(0), gate(1), gate(2), modspec(2), modspec(4), modspec(3),
                              wspec, vspec, wspec, wspec, wspec, wspec,
                              pl.BlockSpec((1, 1, d), lambda b, t: (0, 0, 0)),
                              pl.BlockSpec((d, ne), lambda b, t: (0, 0)),
                              pl.BlockSpec((1, ne), lambda b, t: (0, 0))],
        out_specs=[act, act, pl.BlockSpec((lt, bb, ne), lambda b, t: (t, b, 0))],
        out_shape=(jax.ShapeDtypeStruct((seq, batch, d), F32),
                   jax.ShapeDtypeStruct((seq, batch, d), F32),
                   jax.ShapeDtypeStruct((seq, batch, ne), F32)),
        compiler_params=_cp(("arbitrary", "arbitrary")),
    )(x, y_s5, y_lru, y_rw, z, z, z, mod, mod, mod, w_glu, b_glu.reshape(1, d), wb1, wb2, wb3, w_out,
      norm2_g.reshape(1, 1, d), wr_p, br_p)


ROUTE_IDX, ROUTE_RANK, ROUTE_WT = 0, TOP_K, 2 * TOP_K


def _route_kernel(lg_ref, rec_ref, cnt_ref, run_s):
    tt, nl = lg_ref.shape

    @pl.when(pl.program_id(0) == 0)
    def _():
        run_s[...] = jnp.zeros_like(run_s)

    lane = lax.broadcasted_iota(jnp.int32, (tt, nl), 1).astype(F32)
    neg = jnp.float32(-jnp.inf)
    vals = jnp.where(lane < N_EXPERTS, lg_ref[...], neg)
    tops, hots = [], []
    for _ in range(TOP_K):
        m = jnp.max(vals, axis=-1, keepdims=True)
        idx = jnp.min(jnp.where(vals == m, lane, float(nl)), axis=-1, keepdims=True)
        hot = lane == idx
        vals = jnp.where(hot, neg, vals)
        tops.append((m, idx))
        hots.append(hot)
    es = [jnp.exp(m - tops[0][0]) for m, _ in tops]
    den = es[0]
    for e in es[1:]:
        den = den + e

    mask = jnp.zeros((tt, nl), F32)
    for hot in hots:
        mask = jnp.where(hot, 1.0, mask)
    row = lax.broadcasted_iota(jnp.int32, (tt, tt), 0)
    col = lax.broadcasted_iota(jnp.int32, (tt, tt), 1)
    tri = jnp.where(col < row, 1.0, 0.0).astype(BF16)
    prefix = jnp.dot(tri, mask.astype(BF16), preferred_element_type=F32) + run_s[...]
    run_s[...] = run_s[...] + jnp.sum(mask, axis=0, keepdims=True)
    cnt_ref[...] = run_s[...]

    rec = jnp.zeros((tt, nl), F32)
    for k in range(TOP_K):
        rank = jnp.sum(jnp.where(hots[k], prefix, 0.0), axis=-1, keepdims=True)
        rec = jnp.where(lane == ROUTE_IDX + k, tops[k][1].astype(F32), rec)
        rec = jnp.where(lane == ROUTE_RANK + k, rank, rec)
        rec = jnp.where(lane == ROUTE_WT + k, es[k] / den, rec)
    rec_ref[...] = rec


def _route(logits):
    t, nl = logits.shape
    tt = min(ROUTE_TILE, t)
    while t % tt:
        tt -= SUBLANES
    return pl.pallas_call(
        _route_kernel, name="moe_route",
        grid=(t // tt,),
        in_specs=[pl.BlockSpec((tt, nl), lambda i: (i, 0))],
        out_specs=[pl.BlockSpec((tt, nl), lambda i: (i, 0)), pl.BlockSpec((1, nl), lambda i: (0, 0))],
        out_shape=(jax.ShapeDtypeStruct((t, nl), F32), jax.ShapeDtypeStruct((1, nl), F32)),
        scratch_shapes=[pltpu.VMEM((1, nl), F32)],
        compiler_params=_cp(("arbitrary",)),
    )(logits)


def _expert_kernel(te_ref, tv_ref, xs_ref, wgu_ref, bgu_ref, wd_ref, bd_ref, o_ref, wgu_s, wd_s):
    i = pl.program_id(0)
    de = wd_s.shape[0]
    changed = jnp.logical_or(i == 0, te_ref[i] != te_ref[jnp.maximum(i - 1, 0)])

    @pl.when(changed)
    def _():
        wgu_s[...] = wgu_ref[0].astype(BF16)
        wd_s[...] = wd_ref[0].astype(BF16)

    @pl.when(tv_ref[i] > 0)
    def _():
        gu = jnp.dot(xs_ref[...].astype(BF16), wgu_s[...], preferred_element_type=F32) + bgu_ref[0]
        glu = jnp.minimum(gu[:, :de], SWIGLU_LIMIT)
        lin = jnp.clip(gu[:, de:], -SWIGLU_LIMIT, SWIGLU_LIMIT)
        act = glu * _sigmoid(SWIGLU_ALPHA * glu) * (lin + 1.0)
        o_ref[...] = jnp.dot(act.astype(BF16), wd_s[...], preferred_element_type=F32) + bd_ref[0]

    @pl.when(tv_ref[i] == 0)
    def _():
        o_ref[...] = jnp.zeros_like(o_ref)


def _experts(tile_expert, tile_valid, xs, w_gu, b_gu, w_down, b_down):
    rows, d = xs.shape
    ne, _, n2 = w_gu.shape
    de = w_down.shape[1]
    tm = EXPERT_TILE
    grid_spec = pltpu.PrefetchScalarGridSpec(
        num_scalar_prefetch=2,
        grid=(rows // tm,),
        in_specs=[pl.BlockSpec((tm, d), lambda i, te, tv: (i, 0)),
                  pl.BlockSpec((1, d, n2), lambda i, te, tv: (te[i], 0, 0)),
                  pl.BlockSpec((1, 1, n2), lambda i, te, tv: (te[i], 0, 0)),
                  pl.BlockSpec((1, de, d), lambda i, te, tv: (te[i], 0, 0)),
                  pl.BlockSpec((1, 1, d), lambda i, te, tv: (te[i], 0, 0))],
        out_specs=pl.BlockSpec((tm, d), lambda i, te, tv: (i, 0)),
        scratch_shapes=[pltpu.VMEM((d, n2), BF16), pltpu.VMEM((de, d), BF16)])
    return pl.pallas_call(
        _expert_kernel, name="moe_experts",
        grid_spec=grid_spec,
        out_shape=jax.ShapeDtypeStruct((rows, d), F32),
        compiler_params=_cp(("arbitrary",)),
    )(tile_expert, tile_valid, xs, w_gu, b_gu.reshape(ne, 1, n2), w_down, b_down.reshape(ne, 1, d))


def _combine_kernel(x_ref, og_ref, rec_ref, gt_ref, fg_ref, x2_ref, *, final):
    d = x_ref.shape[-1]
    rec = rec_ref[...]
    y = rec[:, :, ROUTE_WT:ROUTE_WT + 1] * og_ref[:, :, 0:d]
    for k in range(1, TOP_K):
        y = y + rec[:, :, ROUTE_WT + k:ROUTE_WT + k + 1] * og_ref[:, :, k * d:(k + 1) * d]
    x2 = x_ref[...] + gt_ref[...] * y
    if final:
        x2 = x2 * lax.rsqrt(jnp.mean(x2 * x2, axis=-1, keepdims=True) + RMS_EPS) * fg_ref[...]
    x2_ref[...] = x2


def _combine(x1, og_all, tok_off, rec, mod, final_g, final, tokens=256):
    seq, batch, d = x1.shape
    lt, bb = _tiles(seq, batch, tokens)
    nl = rec.shape[-1]
    t_all = og_all.shape[0]
    if t_all % batch == 0 and tok_off % (batch * lt) == 0:
        og = og_all.reshape(t_all // batch, batch, TOP_K * d)
        t0 = tok_off // (batch * lt)
    else:
        og = og_all[tok_off:tok_off + seq * batch].reshape(seq, batch, TOP_K * d)
        t0 = 0
    return pl.pallas_call(
        functools.partial(_combine_kernel, final=final), name="moe_combine",
        grid=(batch // bb, seq // lt),
        in_specs=[pl.BlockSpec((lt, bb, d), lambda b, t: (t, b, 0)),
                  pl.BlockSpec((lt, bb, TOP_K * d), lambda b, t: (t0 + t, b, 0)),
                  pl.BlockSpec((lt, bb, nl), lambda b, t: (t, b, 0)),
                  pl.BlockSpec((1, bb, d), lambda b, t: (0, b, 5)),
                  pl.BlockSpec((1, 1, d), lambda b, t: (0, 0, 0))],
        out_specs=pl.BlockSpec((lt, bb, d), lambda b, t: (t, b, 0)),
        out_shape=jax.ShapeDtypeStruct((seq, batch, d), F32),
        compiler_params=_cp(("arbitrary", "arbitrary")),
    )(x1, og, rec, mod, final_g.reshape(1, 1, d))


def _pad_in_cols(a, axis):
    d = D_MODEL
    main = lax.slice_in_dim(a, 0, 6 * d, axis=axis)
    lora = lax.slice_in_dim(a, 6 * d, 6 * d + RW_LORA, axis=axis)
    gates = lax.slice_in_dim(a, 6 * d + RW_LORA, 9 * d + RW_LORA, axis=axis)
    pad_shape = list(a.shape)
    pad_shape[axis] = LORA_PAD - RW_LORA
    return jnp.concatenate([main, gates, lora, jnp.zeros(pad_shape, a.dtype)], axis=axis)


def _mix_group(x, mod, st, lp, s5p):
    seq, batch, d = x.shape
    s5_re, s5_im, lru_h, lru_conv, rw_s, rw_shift = st
    z = _in_proj(x, mod, lp["norm1_g"], lp["w_in_p"])

    y_s5, n_s5_re, n_s5_im = _s5_mixer(
        z, s5p["ab_re"], s5p["ab_im"], s5p["bb_re"], s5p["bb_im"], s5p["ct_re"], s5p["ct_im"], s5p["d"],
        s5_re.reshape(batch, S5_LANES), s5_im.reshape(batch, S5_LANES))

    y_lru, n_lru_h, tail = _lru_mixer(
        z, lru_conv.transpose(1, 0, 2), lp["lru_conv_w"], lp["lru_conv_b"], lp["lru_w_a"], lp["lru_b_a"],
        lp["lru_w_x"], lp["lru_b_x"], lp["lru_lam"], lru_h)

    y_rw, n_rw_s = _rwkv(z, rw_shift, rw_s, lp)
    n_rw_shift = jnp.concatenate([z[seq - 1, :, COL_R:COL_R + 3 * d], z[seq - 1, :, COL_LORA:COL_LORA + RW_LORA]],
                                 axis=-1)

    x1, h2, logits = _merge(x, y_s5, y_lru, y_rw, z, mod, lp["s5_w_glu"], lp["s5_b_glu"], lp["w_br_s5"],
                            lp["w_br_lru"], lp["w_br_rw"], lp["w_out"], lp["norm2_g"], lp["wr_p"], lp["br_p"])
    new = (n_s5_re.reshape(batch, S5_GROUPS, S5_STATE), n_s5_im.reshape(batch, S5_GROUPS, S5_STATE),
           n_lru_h, tail.transpose(1, 0, 2), n_rw_s, n_rw_shift)
    return x1, h2, logits, new


def _moe(h2_all, logits_all, lp):
    t, d = h2_all.shape
    tm = EXPERT_TILE
    rec, cnt = _route(logits_all)
    idx = rec[:, ROUTE_IDX:ROUTE_IDX + TOP_K].astype(jnp.int32)
    rank = rec[:, ROUTE_RANK:ROUTE_RANK + TOP_K].astype(jnp.int32)
    counts = cnt[0, :N_EXPERTS].astype(jnp.int32)
    padded = ((counts + tm - 1) // tm) * tm
    ends = jnp.cumsum(padded)
    offs = ends - padded
    pos = offs[idx] + rank
    n_tiles = (t * TOP_K + N_EXPERTS * (tm - 1) + tm - 1) // tm
    rows = n_tiles * tm
    src = jnp.zeros((rows,), jnp.int32).at[pos.reshape(-1)].set(
        jnp.repeat(jnp.arange(t, dtype=jnp.int32), TOP_K), unique_indices=True, mode="promise_in_bounds")
    starts = jnp.arange(n_tiles, dtype=jnp.int32) * tm
    tile_valid = (starts < ends[-1]).astype(jnp.int32)
    owner = lambda row: jnp.sum((ends[None, :] <= row[:, None]).astype(jnp.int32), axis=1)
    last = owner(ends[-1:] - 1)[0]
    tile_expert = jnp.clip(jnp.where(tile_valid > 0, owner(starts), last), 0, N_EXPERTS - 1)
    xs = h2_all.at[src].get(mode="promise_in_bounds")
    os_ = _experts(tile_expert, tile_valid, xs, lp["moe_w_gu"], lp["moe_b_gu"], lp["moe_w_down"], lp["moe_b_down"])
    og = os_.at[pos.reshape(-1)].get(mode="promise_in_bounds").reshape(t, TOP_K * d)
    return og, rec


def kernel(x_prompt, x_sample, state_s5_re, state_s5_im, state_lru_h, cache_lru_conv, state_rwkv, cache_rwkv_shift, c_prompt, c_sample, w_mod, b_mod, norm1_g, w_in, s5_lam_re, s5_lam_im, s5_log_step, s5_b_re, s5_b_im, s5_c_re, s5_c_im, s5_d, s5_w_glu, s5_b_glu, lru_conv_w, lru_conv_b, lru_w_a, lru_b_a, lru_w_x, lru_b_x, lru_lam, rw_mu, rw_w0, rw_w2, rw_a0, rw_a2, rw_g2, rw_k_k, rw_k_a, rw_r_k, rw_ln_w, rw_ln_b, w_br_s5, w_br_lru, w_br_rw, w_out, norm2_g, moe_w_router, moe_b_router, moe_w_gu, moe_b_gu, moe_w_down, moe_b_down, final_g):
    depth = w_mod.shape[0]
    d = D_MODEL
    bp, lp_len = x_prompt.shape[0], x_prompt.shape[1]
    bs, ls_len = x_sample.shape[0], x_sample.shape[1]

    mod_all = _modulation(jnp.concatenate([c_prompt, c_sample], axis=0), w_mod, b_mod)

    ab_re, ab_im, q_re, q_im = _s5_params(s5_lam_re, s5_lam_im, s5_log_step)
    shp = (depth, S5_GROUPS, S5_STATE)
    q_re, q_im = q_re.reshape(shp)[..., None], q_im.reshape(shp)[..., None]
    bbar_re = q_re * s5_b_re - q_im * s5_b_im
    bbar_im = q_re * s5_b_im + q_im * s5_b_re
    per = LANES // S5_GROUP

    xs = [_swap_leading(x_prompt), _swap_leading(x_sample)]
    zeros_like_state = lambda s, b: jnp.zeros((b,) + s.shape[2:], F32)
    sample_states = (state_s5_re, state_s5_im, state_lru_h, cache_lru_conv, state_rwkv, cache_rwkv_shift)
    collected = [tuple([] for _ in sample_states), tuple([] for _ in sample_states)]

    for l in range(depth):
        lp = dict(
            norm1_g=norm1_g[l], w_in_p=_pad_in_cols(w_in[l], 1).astype(BF16),
            lru_conv_w=lru_conv_w[l], lru_conv_b=lru_conv_b[l], lru_w_a=lru_w_a[l], lru_b_a=lru_b_a[l],
            lru_w_x=lru_w_x[l], lru_b_x=lru_b_x[l], lru_lam=lru_lam[l],
            rw_mu=rw_mu[l], rw_w0=rw_w0[l], rw_a0=rw_a0[l],
            rw_w2p=jnp.concatenate([rw_w2[l], jnp.zeros((LANES - RW_W_LORA, d), F32)], axis=0).astype(BF16),
            rw_a2p=jnp.concatenate([jnp.zeros((RW_W_LORA, d), F32), rw_a2[l]], axis=0).astype(BF16),
            rw_g2p=jnp.concatenate([rw_g2[l], jnp.zeros((2 * LANES - RW_G_LORA, d), F32)], axis=0).astype(BF16),
            rw_k_k=rw_k_k[l], rw_k_a=rw_k_a[l], rw_r_k=rw_r_k[l], rw_ln_w=rw_ln_w[l], rw_ln_b=rw_ln_b[l],
            s5_w_glu=s5_w_glu[l].astype(BF16), s5_b_glu=s5_b_glu[l], w_br_s5=w_br_s5[l].astype(BF16),
            w_br_lru=w_br_lru[l].astype(BF16), w_br_rw=w_br_rw[l].astype(BF16), w_out=w_out[l].astype(BF16),
            norm2_g=norm2_g[l],
            wr_p=jnp.concatenate([moe_w_router[l], jnp.zeros((d, LANES - N_EXPERTS), F32)], axis=1),
            br_p=jnp.concatenate([moe_b_router[l], jnp.zeros((LANES - N_EXPERTS,), F32)]).reshape(1, LANES),
            moe_w_gu=moe_w_gu[l], moe_b_gu=moe_b_gu[l], moe_w_down=moe_w_down[l], moe_b_down=moe_b_down[l])
        g0 = l * S5_GROUPS
        s5p = dict(
            ab_re=ab_re[g0:g0 + S5_GROUPS].reshape(1, S5_LANES), ab_im=ab_im[g0:g0 + S5_GROUPS].reshape(1, S5_LANES),
            bb_re=_block_diag(bbar_re[l].transpose(0, 2, 1), per).astype(BF16),
            bb_im=_block_diag(bbar_im[l].transpose(0, 2, 1), per).astype(BF16),
            ct_re=_block_diag(s5_c_re[l].transpose(0, 2, 1), per).astype(BF16),
            ct_im=_block_diag(s5_c_im[l].transpose(0, 2, 1), per).astype(BF16),
            d=s5_d[l].reshape(1, d))

        mods = [mod_all[l, :bp][None], mod_all[l, bp:][None]]
        states = [tuple(zeros_like_state(s, bp) for s in sample_states), tuple(s[l] for s in sample_states)]
        x1s, h2s, lgs = [], [], []
        for gi in range(2):
            x1, h2, lg, new = _mix_group(xs[gi], mods[gi], states[gi], lp, s5p)
            x1s.append(x1)
            h2s.append(h2.reshape(-1, d))
            lgs.append(lg.reshape(-1, LANES))
            for lst, s in zip(collected[gi], new):
                lst.append(s)

        og, rec = _moe(jnp.concatenate(h2s[::-1], axis=0), jnp.concatenate(lgs[::-1], axis=0), lp)
        ts = ls_len * bs
        offs = [ts, 0]
        recs = [rec[ts:].reshape(lp_len, bp, LANES), rec[:ts].reshape(ls_len, bs, LANES)]
        xs = [_combine(x1s[gi], og, offs[gi], recs[gi], mods[gi], final_g, final=(l == depth - 1))
              for gi in range(2)]

    y_prompt = _swap_leading(xs[0])
    y_sample = _swap_leading(xs[1])
    p_states = tuple(jnp.stack(lst) for lst in collected[0])
    s_states = tuple(jnp.stack(lst) for lst in collected[1])
    return (y_prompt, y_sample) + p_states + s_states
```

```python
import functools
import math

import jax
import jax.numpy as jnp
from jax import lax
from jax.experimental import pallas as pl
from jax.experimental.pallas import tpu as pltpu

F32 = jnp.float32
BF16 = jnp.bfloat16
HIGHEST = lax.Precision.HIGHEST

D_MODEL = 1024
RMS_EPS = 1e-5
S5_GROUP = 16
S5_GROUPS = D_MODEL // S5_GROUP
S5_STATE = 64
S5_LANES = S5_GROUPS * S5_STATE
LRU_HEADS = 8
LRU_BLOCK = D_MODEL // LRU_HEADS
CONV_WIDTH = 4
LRU_C = 8.0
RW_HEAD = 64
RW_HEADS = D_MODEL // RW_HEAD
RW_W_LORA = 64
RW_A_LORA = 64
RW_G_LORA = 160
RW_LORA = RW_W_LORA + RW_A_LORA + RW_G_LORA
RW_COLS = 3 * D_MODEL + RW_LORA
RW_LN_EPS = 64e-5
N_EXPERTS = 32
TOP_K = 4
SWIGLU_ALPHA = 1.702
SWIGLU_LIMIT = 7.0

LANES = 128
SUBLANES = 8
VMEM_LIMIT = 56 * 1024 * 1024

LORA_PAD = 512
COL_S5 = 0
COL_XLRU = 1 * D_MODEL
COL_GLRU = 2 * D_MODEL
COL_R = 3 * D_MODEL
COL_K = 4 * D_MODEL
COL_V = 5 * D_MODEL
COL_GATE = 6 * D_MODEL
COL_LORA = 9 * D_MODEL
D_IN_PAD = COL_LORA + LORA_PAD

S5_LANE_BLOCK = LANES
S5_STATE_BLOCK = (LANES // S5_GROUP) * S5_STATE
EXPERT_TILE = 512
ROUTE_TILE = 512


def _cp(sem, vmem=VMEM_LIMIT):
    return pltpu.CompilerParams(dimension_semantics=sem, vmem_limit_bytes=vmem)


def _gelu(x):
    return 0.5 * x * (1.0 + jnp.tanh(math.sqrt(2.0 / math.pi) * (x + 0.044715 * (x * x * x))))


def _sigmoid(x):
    return 1.0 / (1.0 + jnp.exp(-x))


def _softplus(x):
    return jnp.maximum(x, 0.0) + jnp.log1p(jnp.exp(-jnp.abs(x)))


def _tiles(seq, batch, tokens):
    bb = min(batch, max(SUBLANES, (tokens // seq) // SUBLANES * SUBLANES))
    while batch % bb:
        bb -= SUBLANES
    lt = max(1, min(seq, tokens // bb))
    while seq % lt:
        lt -= 1
    return lt, bb


def _swap_kernel(x_ref, o_ref):
    n0, n1, _ = x_ref.shape
    if n0 <= n1:
        for i in range(n0):
            o_ref[:, i, :] = x_ref[i]
    else:
        for i in range(n1):
            o_ref[i] = x_ref[:, i, :]


def _swap_leading(x, tokens=1024):
    a, b, d = x.shape
    ta = min(a, SUBLANES) if a <= b else min(a, max(SUBLANES, tokens // b))
    tb = min(b, max(SUBLANES, tokens // ta))
    while a % ta:
        ta -= SUBLANES
    while b % tb:
        tb -= SUBLANES
    return pl.pallas_call(
        _swap_kernel, name="swap_leading",
        grid=(a // ta, b // tb),
        in_specs=[pl.BlockSpec((ta, tb, d), lambda i, j: (i, j, 0))],
        out_specs=pl.BlockSpec((tb, ta, d), lambda i, j: (j, i, 0)),
        out_shape=jax.ShapeDtypeStruct((b, a, d), x.dtype),
        compiler_params=_cp(("arbitrary", "arbitrary")),
    )(x)


def _mod_kernel(c_ref, w_ref, b_ref, o_ref):
    c = c_ref[...]
    s = c * _sigmoid(c)
    o_ref[0] = jnp.dot(s, w_ref[0], precision=HIGHEST, preferred_element_type=F32) + b_ref[0]


def _modulation(c, w_mod, b_mod):
    depth, d, n = w_mod.shape
    bc = c.shape[0]
    tn = 1536
    return pl.pallas_call(
        _mod_kernel, name="adaln_mod",
        grid=(depth, n // tn),
        in_specs=[pl.BlockSpec((bc, d), lambda l, j: (0, 0)),
                  pl.BlockSpec((1, d, tn), lambda l, j: (l, 0, j)),
                  pl.BlockSpec((1, 1, tn), lambda l, j: (l, 0, j))],
        out_specs=pl.BlockSpec((1, bc, tn), lambda l, j: (l, 0, j)),
        out_shape=jax.ShapeDtypeStruct((depth, bc, n), F32),
        compiler_params=_cp(("arbitrary", "arbitrary")),
    )(c, w_mod, b_mod.reshape(depth, 1, n))


def _in_kernel(x_ref, sh_ref, sc_ref, g_ref, w_ref, o_ref, h_scr):
    lt, bb, d = x_ref.shape

    @pl.when(pl.program_id(2) == 0)
    def _():
        x = x_ref[...]
        y = x * lax.rsqrt(jnp.mean(x * x, axis=-1, keepdims=True) + RMS_EPS) * g_ref[...]
        h = y * (1.0 + sc_ref[...]) + sh_ref[...]
        h_scr[...] = h.reshape(lt * bb, d).astype(BF16)

    o = jnp.dot(h_scr[...], w_ref[...], preferred_element_type=F32)
    o_ref[...] = o.reshape(lt, bb, o.shape[-1])


def _in_proj(x, mod, norm_g, w_in_p, tokens=1024, tn=512):
    seq, batch, d = x.shape
    lt, bb = _tiles(seq, batch, tokens)
    n = w_in_p.shape[1]
    return pl.pallas_call(
        _in_kernel, name="in_proj",
        grid=(batch // bb, seq // lt, n // tn),
        in_specs=[pl.BlockSpec((lt, bb, d), lambda b, t, j: (t, b, 0)),
                  pl.BlockSpec((1, bb, d), lambda b, t, j: (0, b, 0)),
                  pl.BlockSpec((1, bb, d), lambda b, t, j: (0, b, 1)),
                  pl.BlockSpec((1, 1, d), lambda b, t, j: (0, 0, 0)),
                  pl.BlockSpec((d, tn), lambda b, t, j: (0, j))],
        out_specs=pl.BlockSpec((lt, bb, tn), lambda b, t, j: (t, b, j)),
        out_shape=jax.ShapeDtypeStruct((seq, batch, n), F32),
        scratch_shapes=[pltpu.VMEM((lt * bb, d), BF16)],
        compiler_params=_cp(("arbitrary", "arbitrary", "arbitrary")),
    )(x, mod, mod, norm_g.reshape(1, 1, d), w_in_p)


def _s5_kernel(u_ref, bbr_ref, bbi_ref, ar_ref, ai_ref, ctr_ref, cti_ref, d_ref, h0r_ref, h0i_ref,
               y_ref, hr_ref, hi_ref, sr, si):
    lt, bb, nl = u_ref.shape
    ns = sr.shape[-1]

    @pl.when(pl.program_id(2) == 0)
    def _():
        hr_ref[...] = h0r_ref[...]
        hi_ref[...] = h0i_ref[...]

    u2 = u_ref[...].reshape(lt * bb, nl)
    ub = u2.astype(BF16)
    sr[...] = jnp.dot(ub, bbr_ref[0], preferred_element_type=F32).reshape(lt, bb, ns)
    si[...] = jnp.dot(ub, bbi_ref[0], preferred_element_type=F32).reshape(lt, bb, ns)
    ar = jnp.broadcast_to(ar_ref[...], (bb, ns))
    ai = jnp.broadcast_to(ai_ref[...], (bb, ns))

    def body(i, carry):
        hr, hi = carry
        nr = ar * hr - ai * hi + sr[i]
        ni = ar * hi + ai * hr + si[i]
        sr[i] = nr
        si[i] = ni
        return nr, ni

    hr, hi = lax.fori_loop(0, lt, body, (hr_ref[...], hi_ref[...]), unroll=min(lt, 8))
    hr_ref[...] = hr
    hi_ref[...] = hi
    y = (jnp.dot(sr[...].reshape(lt * bb, ns).astype(BF16), ctr_ref[0], preferred_element_type=F32)
         - jnp.dot(si[...].reshape(lt * bb, ns).astype(BF16), cti_ref[0], preferred_element_type=F32)
         + d_ref[...] * u2)
    y_ref[...] = y.reshape(lt, bb, nl)


def _s5_params_kernel(lr_ref, li_ref, ls_ref, abr_ref, abi_ref, qr_ref, qi_ref):
    lr = lr_ref[...]
    li = li_ref[...]
    step = jnp.exp(ls_ref[...])
    mag = jnp.exp(lr * step)
    ab_re = mag * jnp.cos(li * step)
    ab_im = mag * jnp.sin(li * step)
    den = lr * lr + li * li
    abr_ref[...] = ab_re
    abi_ref[...] = ab_im
    qr_ref[...] = ((ab_re - 1.0) * lr + ab_im * li) / den
    qi_ref[...] = (ab_im * lr - (ab_re - 1.0) * li) / den


def _s5_params(lam_re, lam_im, log_step):
    depth, g, n = lam_re.shape
    shp = jax.ShapeDtypeStruct((depth * g, n), F32)
    return pl.pallas_call(_s5_params_kernel, name="s5_params", out_shape=(shp, shp, shp, shp))(
        lam_re.reshape(depth * g, n), lam_im.reshape(depth * g, n), log_step.reshape(depth * g, 1))


def _block_diag(blocks, per):
    g, a, b = blocks.shape
    x = blocks.reshape(g // per, per, a, b)
    eye = jnp.eye(per, dtype=blocks.dtype)
    return jnp.einsum("gpab,pq->gpaqb", x, eye).reshape(g // per, per * a, per * b)


def _s5_mixer(z, ab_re, ab_im, bb_re, bb_im, ct_re, ct_im, d_skip, h0_re, h0_im, tokens=512):
    seq, batch, _ = z.shape
    lt, bb = _tiles(seq, batch, tokens)
    nl, ns = S5_LANE_BLOCK, S5_STATE_BLOCK
    nblk = D_MODEL // nl
    col0 = COL_S5 // nl
    y, hr, hi = pl.pallas_call(
        _s5_kernel, name="s5_mixer",
        grid=(batch // bb, nblk, seq // lt),
        in_specs=[pl.BlockSpec((lt, bb, nl), lambda b, j, t: (t, b, col0 + j)),
                  pl.BlockSpec((1, nl, ns), lambda b, j, t: (j, 0, 0)),
                  pl.BlockSpec((1, nl, ns), lambda b, j, t: (j, 0, 0)),
                  pl.BlockSpec((1, ns), lambda b, j, t: (0, j)),
                  pl.BlockSpec((1, ns), lambda b, j, t: (0, j)),
                  pl.BlockSpec((1, ns, nl), lambda b, j, t: (j, 0, 0)),
                  pl.BlockSpec((1, ns, nl), lambda b, j, t: (j, 0, 0)),
                  pl.BlockSpec((1, nl), lambda b, j, t: (0, j)),
                  pl.BlockSpec((bb, ns), lambda b, j, t: (b, j)),
                  pl.BlockSpec((bb, ns), lambda b, j, t: (b, j))],
        out_specs=[pl.BlockSpec((lt, bb, nl), lambda b, j, t: (t, b, j)),
                   pl.BlockSpec((bb, ns), lambda b, j, t: (b, j)),
                   pl.BlockSpec((bb, ns), lambda b, j, t: (b, j))],
        out_shape=(jax.ShapeDtypeStruct((seq, batch, D_MODEL), F32),
                   jax.ShapeDtypeStruct((batch, S5_LANES), F32),
                   jax.ShapeDtypeStruct((batch, S5_LANES), F32)),
        scratch_shapes=[pltpu.VMEM((lt, bb, ns), F32), pltpu.VMEM((lt, bb, ns), F32)],
        compiler_params=_cp(("arbitrary", "arbitrary", "arbitrary")),
    )(z, bb_re, bb_im, ab_re, ab_im, ct_re, ct_im, d_skip, h0_re, h0_im)
    return y, hr, hi


def _lru_kernel(x_ref, g_ref, cbuf_ref, cw_ref, cb_ref, wa_ref, ba_ref, wx_ref, bx_ref, lam_ref, h0_ref,
                y_ref, hn_ref, tail_ref, prev_s, a_s, h_s):
    lt, bb, w = x_ref.shape
    taps = cw_ref.shape[0]

    @pl.when(pl.program_id(2) == 0)
    def _():
        hn_ref[...] = h0_ref[...]
        prev_s[...] = cbuf_ref[...]

    xp = jnp.concatenate([prev_s[...], x_ref[...]], axis=0)
    xc = cb_ref[...] + xp[0:lt] * cw_ref[0]
    for tap in range(1, taps):
        xc = xc + xp[tap:tap + lt] * cw_ref[tap]
    prev_s[...] = xp[lt:lt + taps - 1]
    tail_ref[...] = xp[lt:lt + taps - 1]

    xc2 = xc.reshape(lt * bb, w)
    xb = xc2.astype(BF16)
    r = _sigmoid(jnp.dot(xb, wa_ref[0], preferred_element_type=F32) + ba_ref[...])
    i = _sigmoid(jnp.dot(xb, wx_ref[0], preferred_element_type=F32) + bx_ref[...])
    log_a = (-LRU_C) * r * _softplus(-lam_ref[...])
    a = jnp.exp(log_a)
    th = jnp.tanh(log_a)
    neg_expm1 = -2.0 * th / (1.0 - th)
    b = jnp.sqrt(neg_expm1) * (i * xc2)
    a_s[...] = a.reshape(lt, bb, w)
    h_s[...] = b.reshape(lt, bb, w)

    def body(t, h):
        h = a_s[t] * h + h_s[t]
        h_s[t] = h
        return h

    hn_ref[...] = lax.fori_loop(0, lt, body, hn_ref[...], unroll=min(lt, 8))
    y_ref[...] = h_s[...] * _gelu(g_ref[...])


def _lru_mixer(z, conv_buf_t, conv_w, conv_b, w_a, b_a, w_x, b_x, lam, h0, tokens=1024):
    seq, batch, _ = z.shape
    lt, bb = _tiles(seq, batch, tokens)
    w = LRU_BLOCK
    xcol, gcol = COL_XLRU // w, COL_GLRU // w
    taps = CONV_WIDTH
    vec = lambda a: a.reshape(1, D_MODEL)
    vspec = pl.BlockSpec((1, w), lambda b, h, t: (0, h))
    y, hn, tail = pl.pallas_call(
        _lru_kernel, name="rglru_mixer",
        grid=(batch // bb, LRU_HEADS, seq // lt),
        in_specs=[pl.BlockSpec((lt, bb, w), lambda b, h, t: (t, b, xcol + h)),
                  pl.BlockSpec((lt, bb, w), lambda b, h, t: (t, b, gcol + h)),
                  pl.BlockSpec((taps - 1, bb, w), lambda b, h, t: (0, b, h)),
                  pl.BlockSpec((taps, 1, w), lambda b, h, t: (0, 0, h)),
                  vspec,
                  pl.BlockSpec((1, w, w), lambda b, h, t: (h, 0, 0)),
                  vspec,
                  pl.BlockSpec((1, w, w), lambda b, h, t: (h, 0, 0)),
                  vspec, vspec,
                  pl.BlockSpec((bb, w), lambda b, h, t: (b, h))],
        out_specs=[pl.BlockSpec((lt, bb, w), lambda b, h, t: (t, b, h)),
                   pl.BlockSpec((bb, w), lambda b, h, t: (b, h)),
                   pl.BlockSpec((taps - 1, bb, w), lambda b, h, t: (0, b, h))],
        out_shape=(jax.ShapeDtypeStruct((seq, batch, D_MODEL), F32),
                   jax.ShapeDtypeStruct((batch, D_MODEL), F32),
                   jax.ShapeDtypeStruct((taps - 1, batch, D_MODEL), F32)),
        scratch_shapes=[pltpu.VMEM((taps - 1, bb, w), F32), pltpu.VMEM((lt, bb, w), F32),
                        pltpu.VMEM((lt, bb, w), F32)],
        compiler_params=_cp(("arbitrary", "arbitrary", "arbitrary")),
    )(z, z, conv_buf_t, conv_w.reshape(taps, 1, D_MODEL), vec(conv_b), w_a.astype(BF16), vec(b_a),
      w_x.astype(BF16), vec(b_x), vec(lam), h0)
    return y, hn, tail


def _rwkv_kernel(zr_ref, zk_ref, zv_ref, zl_ref, sh_ref, shl_ref, mu_ref, mul_ref, w0_ref, w2_ref, a0_ref, a2_ref,
                 g2_ref, kk_ref, ka_ref, rk_ref, lnw_ref, lnb_ref, s0_ref,
                 y_ref, sn_ref,
                 prev_s, prevl_s, s_scr, sa_s, nkk_s, r_s, w_s, k_s, v_s, b_s, y_s):
    lt, bb, wd = zr_ref.shape
    kp = wd // LANES
    hd = RW_HEAD
    nl = zl_ref.shape[-1]
    n = lt * bb
    ti = pl.program_id(2)

    @pl.when(ti == 0)
    def _():
        prev_s[...] = sh_ref[...]
        prevl_s[...] = shl_ref[...]
        for i in range(hd):
            rows = [s0_ref[:, 2 * c + par, i, :] for par in range(2) for c in range(kp)]
            s_scr[:, i, :] = jnp.concatenate(rows, axis=0).T

    def shifted(z_ref, prev, mu):
        z = z_ref[...]
        zp = jnp.concatenate([prev, z[0:lt - 1]], axis=0) if lt > 1 else prev
        return z + (zp - z) * mu, z[lt - 1:lt]

    r, last_r = shifted(zr_ref, prev_s[0:1], mu_ref[0:1])
    k, last_k = shifted(zk_ref, prev_s[1:2], mu_ref[1:2])
    v, last_v = shifted(zv_ref, prev_s[2:3], mu_ref[2:3])
    lo, last_l = shifted(zl_ref, prevl_s[...], mul_ref[...])
    prev_s[0:1] = last_r
    prev_s[1:2] = last_k
    prev_s[2:3] = last_v
    prevl_s[...] = last_l

    lo2 = lo.reshape(n, nl)
    wa_in = lo2[:, 0:LANES]
    g_in = lo2[:, LANES:3 * LANES]
    wpre = w0_ref[...] + jnp.dot(jnp.tanh(wa_in).astype(BF16), w2_ref[...], preferred_element_type=F32)
    decay = jnp.exp(-jnp.exp(-_softplus(-wpre) - 0.5))
    a = _sigmoid(a0_ref[...] + jnp.dot(wa_in.astype(BF16), a2_ref[...], preferred_element_type=F32))
    g = jnp.dot(_sigmoid(g_in).astype(BF16), g2_ref[...], preferred_element_type=F32)
    k2 = k.reshape(n, wd)
    shp = (lt, bb, wd)

    def to_lanes(q):
        rows = jnp.concatenate([q[:, :, c * LANES:(c + 1) * LANES] for c in range(kp)], axis=1) if kp > 1 else q
        t = jnp.swapaxes(rows, 1, 2)
        return jnp.concatenate([t[:, 0:hd, :], t[:, hd:2 * hd, :]], axis=2)

    r_s[...] = to_lanes(r)
    w_s[...] = to_lanes(decay.reshape(shp))
    k_s[...] = to_lanes((k2 * (1.0 + (a - 1.0) * ka_ref[...])).reshape(shp))
    v_s[...] = to_lanes(v)
    kk = to_lanes((k2 * kk_ref[...]).reshape(shp))
    kkn = kk * lax.rsqrt(jnp.maximum(jnp.sum(kk * kk, axis=1, keepdims=True), 1e-24))
    nkk_s[0:lt] = -kkn
    nkk_s[lt:lt + 1] = jnp.zeros((1, hd, LANES), F32)
    b_s[...] = kkn * to_lanes(a.reshape(shp))

    acc0 = [jnp.zeros((hd, LANES), F32) for _ in range(2)]
    for j in range(hd):
        acc0[j % 2] = acc0[j % 2] + s_scr[j] * nkk_s[0, pl.ds(j, 1), :]
    sa_s[...] = acc0[0] + acc0[1]
    rg = hd // 2

    def step(t, carry):
        row = lambda ref, tt, j: ref[tt, pl.ds(j, 1), :]

        def group(g, c2):
            rows = pl.ds(pl.multiple_of(g * rg, rg), rg)
            sa = sa_s[rows, :]
            vt = v_s[t, rows, :]
            nac = [jnp.zeros((rg, LANES), F32) for _ in range(2)]
            yac = [jnp.zeros((rg, LANES), F32) for _ in range(2)]
            for j in range(hd):
                sj = s_scr[j, rows, :] * row(w_s, t, j) + sa * row(b_s, t, j) + vt * row(k_s, t, j)
                s_scr[j, rows, :] = sj
                yac[j % 2] = yac[j % 2] + sj * row(r_s, t, j)
                nac[j % 2] = nac[j % 2] + sj * row(nkk_s, t + 1, j)
            y_s[t, rows, :] = yac[0] + yac[1]
            sa_s[rows, :] = nac[0] + nac[1]
            return c2

        lax.fori_loop(0, hd // rg, group, 0)
        return carry

    lax.fori_loop(0, lt, step, 0)

    ys = y_s[...]
    mean = jnp.mean(ys, axis=1, keepdims=True)
    yc = ys - mean
    var = jnp.mean(yc * yc, axis=1, keepdims=True)
    yn = yc * lax.rsqrt(var + RW_LN_EPS)
    bonus = jnp.sum(r_s[...] * k_s[...] * rk_ref[...], axis=1, keepdims=True)
    o = yn * lnw_ref[...] + lnb_ref[...] + bonus * v_s[...]
    o = jnp.swapaxes(jnp.concatenate([o[:, :, 0:hd], o[:, :, hd:2 * hd]], axis=1), 1, 2)
    nat = jnp.concatenate([o[:, c * bb:(c + 1) * bb, :] for c in range(kp)], axis=2) if kp > 1 else o
    y_ref[...] = nat * g.reshape(shp)

    @pl.when(ti == pl.num_programs(2) - 1)
    def _():
        for i in range(hd):
            m = s_scr[:, i, :].T
            for par in range(2):
                for c in range(kp):
                    r0 = (par * kp + c) * bb
                    sn_ref[:, 2 * c + par, i, :] = m[r0:r0 + bb, :]


def _rwkv(z, rw_shift, rw_s, lp, lt=32):
    seq, batch, _ = z.shape
    d, hd, nl = D_MODEL, RW_HEAD, LORA_PAD
    bb, kp = _rw_tiles(batch)
    wd = LANES * kp
    n_pg = d // wd
    lt = min(lt, seq)
    while seq % lt:
        lt -= 1
    pad = lambda a: jnp.concatenate([a, jnp.zeros(a.shape[:-1] + (nl - RW_LORA,), a.dtype)], axis=-1)
    sh = rw_shift[:, 0:3 * d].reshape(batch, 3, d).transpose(1, 0, 2)
    shl = pad(rw_shift[:, 3 * d:])[None]
    mu = lp["rw_mu"][0:3 * d].reshape(3, 1, d)
    mul = pad(lp["rw_mu"][3 * d:]).reshape(1, 1, nl)
    vec = lambda a: a.reshape(1, d)
    zspec = lambda col: pl.BlockSpec((lt, bb, wd), lambda b, g, t: (t, b, col // wd + g))
    vspec = pl.BlockSpec((1, wd), lambda b, g, t: (0, g))
    wspec = lambda rows: pl.BlockSpec((rows, wd), lambda b, g, t: (0, g))
    pspec = pl.BlockSpec((1, hd, LANES), lambda b, g, t: (g, 0, 0))
    sspec = pl.BlockSpec((bb, 2 * kp, hd, hd), lambda b, g, t: (b, g, 0, 0))
    chunk = pltpu.VMEM((lt, hd, LANES), F32)
    y, sn = pl.pallas_call(
        _rwkv_kernel, name="rwkv7_mixer",
        grid=(batch // bb, n_pg, seq // lt),
        in_specs=[zspec(COL_R), zspec(COL_K), zspec(COL_V),
                  pl.BlockSpec((lt, bb, nl), lambda b, g, t: (t, b, COL_LORA // nl)),
                  pl.BlockSpec((3, bb, wd), lambda b, g, t: (0, b, g)),
                  pl.BlockSpec((1, bb, nl), lambda b, g, t: (0, b, 0)),
                  pl.BlockSpec((3, 1, wd), lambda b, g, t: (0, 0, g)),
                  pl.BlockSpec((1, 1, nl), lambda b, g, t: (0, 0, 0)),
                  vspec, wspec(LANES), vspec, wspec(LANES), wspec(2 * LANES), vspec, vspec,
                  pspec, pspec, pspec, sspec],
        out_specs=[pl.BlockSpec((lt, bb, wd), lambda b, g, t: (t, b, g)), sspec],
        out_shape=(jax.ShapeDtypeStruct((seq, batch, d), F32),
                   jax.ShapeDtypeStruct((batch, RW_HEADS, hd, hd), F32)),
        scratch_shapes=[pltpu.VMEM((3, bb, wd), F32), pltpu.VMEM((1, bb, nl), F32),
                        pltpu.VMEM((hd, hd, LANES), F32), pltpu.VMEM((hd, LANES), F32),
                        pltpu.VMEM((lt + 1, hd, LANES), F32)] + [chunk] * 6,
        compiler_params=_cp(("arbitrary", "arbitrary", "arbitrary")),
    )(z, z, z, z, sh, shl, mu, mul, vec(lp["rw_w0"]), lp["rw_w2p"], vec(lp["rw_a0"]), lp["rw_a2p"], lp["rw_g2p"],
      vec(lp["rw_k_k"]), vec(lp["rw_k_a"]), _rw_param_blocks(lp["rw_r_k"].reshape(-1), bb, kp),
      _rw_param_blocks(lp["rw_ln_w"], bb, kp), _rw_param_blocks(lp["rw_ln_b"], bb, kp), rw_s)
    return y, sn


def _rw_tiles(batch):
    bb = min(batch, LANES // 2)
    assert (LANES // 2) % bb == 0 and batch % bb == 0 and bb % SUBLANES == 0, batch
    return bb, (LANES // 2) // bb


def _rw_param_blocks(p, bb, kp):
    n_pg = RW_HEADS // (2 * kp)
    x = p.reshape(n_pg, kp, 2, RW_HEAD).transpose(0, 3, 2, 1)
    return jnp.broadcast_to(x[..., None], (n_pg, RW_HEAD, 2, kp, bb)).reshape(n_pg, RW_HEAD, LANES)


def _merge_kernel(x_ref, ys5_ref, ylru_ref, yrw_ref, zg1_ref, zg2_ref, zg3_ref,
                  gt1_ref, sc2_ref, sh2_ref, wglu_ref, bglu_ref, wb1_ref, wb2_ref, wb3_ref, wout_ref,
                  n2g_ref, wr_ref, br_ref,
                  x1_ref, h2_ref, lg_ref):
    lt, bb, d = x_ref.shape
    n = lt * bb
    mm = lambda a, w_ref: jnp.dot(a.astype(BF16), w_ref[...], preferred_element_type=F32)
    flat = lambda ref: ref[...].reshape(n, d)

    y1 = _gelu(flat(ys5_ref))
    y1 = y1 * _sigmoid(mm(y1, wglu_ref) + bglu_ref[...])
    y3 = flat(yrw_ref)
    merged = (_sigmoid(flat(zg1_ref)) * mm(y1, wb1_ref)
              + _sigmoid(flat(zg2_ref)) * mm(flat(ylru_ref), wb2_ref)
              + _sigmoid(flat(zg3_ref)) * mm(y3, wb3_ref))
    upd = mm(merged, wout_ref).reshape(lt, bb, d)
    x1 = x_ref[...] + gt1_ref[...] * upd
    x1_ref[...] = x1
    y = x1 * lax.rsqrt(jnp.mean(x1 * x1, axis=-1, keepdims=True) + RMS_EPS) * n2g_ref[...]
    h2 = y * (1.0 + sc2_ref[...]) + sh2_ref[...]
    h2_ref[...] = h2
    logits = jnp.dot(h2.reshape(n, d), wr_ref[...], precision=HIGHEST, preferred_element_type=F32)
    lg_ref[...] = (logits + br_ref[...]).reshape(lt, bb, lg_ref.shape[-1])


def _merge(x, y_s5, y_lru, y_rw, z, mod, w_glu, b_glu, wb1, wb2, wb3, w_out, norm2_g, wr_p, br_p,
           tokens=256):
    seq, batch, d = x.shape
    lt, bb = _tiles(seq, batch, tokens)
    act = pl.BlockSpec((lt, bb, d), lambda b, t: (t, b, 0))
    gate = lambda i: pl.BlockSpec((lt, bb, d), lambda b, t: (t, b, COL_GATE // d + i))
    modspec = lambda i: pl.BlockSpec((1, bb, d), lambda b, t: (0, b, i))
    wspec = pl.BlockSpec((d, d), lambda b, t: (0, 0), pipeline_mode=pl.Buffered(1))
    vspec = pl.BlockSpec((1, d), lambda b, t: (0, 0))
    ne = wr_p.shape[1]
    return pl.pallas_call(
        _merge_kernel, name="merge_norm2_router",
        grid=(batch // bb, seq // lt),
        in_specs=[act] * 4 + [gate(0), gate(1), gate(2), modspec(2), modspec(4), modspec(3),
                              wspec, vspec, wspec, wspec, wspec, wspec,
                              pl.BlockSpec((1, 1, d), lambda b, t: (0, 0, 0)),
                              pl.BlockSpec((d, ne), lambda b, t: (0, 0)),
                              pl.BlockSpec((1, ne), lambda b, t: (0, 0))],
        out_specs=[act, act, pl.BlockSpec((lt, bb, ne), lambda b, t: (t, b, 0))],
        out_shape=(jax.ShapeDtypeStruct((seq, batch, d), F32),
                   jax.ShapeDtypeStruct((seq, batch, d), F32),
                   jax.ShapeDtypeStruct((seq, batch, ne), F32)),
        compiler_params=_cp(("arbitrary", "arbitrary")),
    )(x, y_s5, y_lru, y_rw, z, z, z, mod, mod, mod, w_glu, b_glu.reshape(1, d), wb1, wb2, wb3, w_out,
      norm2_g.reshape(1, 1, d), wr_p, br_p)


ROUTE_IDX, ROUTE_RANK, ROUTE_WT = 0, TOP_K, 2 * TOP_K


def _route_kernel(lg_ref, rec_ref, cnt_ref, run_s):
    tt, nl = lg_ref.shape

    @pl.when(pl.program_id(0) == 0)
    def _():
        run_s[...] = jnp.zeros_like(run_s)

    lane = lax.broadcasted_iota(jnp.int32, (tt, nl), 1).astype(F32)
    neg = jnp.float32(-jnp.inf)
    vals = jnp.where(lane < N_EXPERTS, lg_ref[...], neg)
    tops, hots = [], []
    for _ in range(TOP_K):
        m = jnp.max(vals, axis=-1, keepdims=True)
        idx = jnp.min(jnp.where(vals == m, lane, float(nl)), axis=-1, keepdims=True)
        hot = lane == idx
        vals = jnp.where(hot, neg, vals)
        tops.append((m, idx))
        hots.append(hot)
    es = [jnp.exp(m - tops[0][0]) for m, _ in tops]
    den = es[0]
    for e in es[1:]:
        den = den + e

    mask = jnp.zeros((tt, nl), F32)
    for hot in hots:
        mask = jnp.where(hot, 1.0, mask)
    row = lax.broadcasted_iota(jnp.int32, (tt, tt), 0)
    col = lax.broadcasted_iota(jnp.int32, (tt, tt), 1)
    tri = jnp.where(col < row, 1.0, 0.0).astype(BF16)
    prefix = jnp.dot(tri, mask.astype(BF16), preferred_element_type=F32) + run_s[...]
    run_s[...] = run_s[...] + jnp.sum(mask, axis=0, keepdims=True)
    cnt_ref[...] = run_s[...]

    rec = jnp.zeros((tt, nl), F32)
    for k in range(TOP_K):
        rank = jnp.sum(jnp.where(hots[k], prefix, 0.0), axis=-1, keepdims=True)
        rec = jnp.where(lane == ROUTE_IDX + k, tops[k][1].astype(F32), rec)
        rec = jnp.where(lane == ROUTE_RANK + k, rank, rec)
        rec = jnp.where(lane == ROUTE_WT + k, es[k] / den, rec)
    rec_ref[...] = rec


def _route(logits):
    t, nl = logits.shape
    tt = min(ROUTE_TILE, t)
    while t % tt:
        tt -= SUBLANES
    return pl.pallas_call(
        _route_kernel, name="moe_route",
        grid=(t // tt,),
        in_specs=[pl.BlockSpec((tt, nl), lambda i: (i, 0))],
        out_specs=[pl.BlockSpec((tt, nl), lambda i: (i, 0)), pl.BlockSpec((1, nl), lambda i: (0, 0))],
        out_shape=(jax.ShapeDtypeStruct((t, nl), F32), jax.ShapeDtypeStruct((1, nl), F32)),
        scratch_shapes=[pltpu.VMEM((1, nl), F32)],
        compiler_params=_cp(("arbitrary",)),
    )(logits)


def _expert_kernel(te_ref, tv_ref, xs_ref, wgu_ref, bgu_ref, wd_ref, bd_ref, o_ref, wgu_s, wd_s):
    i = pl.program_id(0)
    de = wd_s.shape[0]
    changed = jnp.logical_or(i == 0, te_ref[i] != te_ref[jnp.maximum(i - 1, 0)])

    @pl.when(changed)
    def _():
        wgu_s[...] = wgu_ref[0, 0].astype(BF16)
        wd_s[...] = wd_ref[0, 0].astype(BF16)

    @pl.when(tv_ref[i] > 0)
    def _():
        gu = jnp.dot(xs_ref[...].astype(BF16), wgu_s[...], preferred_element_type=F32) + bgu_ref[0, 0]
        glu = jnp.minimum(gu[:, :de], SWIGLU_LIMIT)
        lin = jnp.clip(gu[:, de:], -SWIGLU_LIMIT, SWIGLU_LIMIT)
        act = glu * _sigmoid(SWIGLU_ALPHA * glu) * (lin + 1.0)
        o_ref[...] = jnp.dot(act.astype(BF16), wd_s[...], preferred_element_type=F32) + bd_ref[0, 0]

    @pl.when(tv_ref[i] == 0)
    def _():
        o_ref[...] = jnp.zeros_like(o_ref)


def _experts(tile_expert, tile_valid, xs, layer, w_gu, b_gu, w_down, b_down):
    rows, d = xs.shape
    depth, ne, _, n2 = w_gu.shape
    de = w_down.shape[2]
    tm = EXPERT_TILE
    grid_spec = pltpu.PrefetchScalarGridSpec(
        num_scalar_prefetch=2,
        grid=(rows // tm,),
        in_specs=[pl.BlockSpec((tm, d), lambda i, te, tv: (i, 0)),
                  pl.BlockSpec((1, 1, d, n2), lambda i, te, tv: (layer, te[i], 0, 0)),
                  pl.BlockSpec((1, 1, 1, n2), lambda i, te, tv: (layer, te[i], 0, 0)),
                  pl.BlockSpec((1, 1, de, d), lambda i, te, tv: (layer, te[i], 0, 0)),
                  pl.BlockSpec((1, 1, 1, d), lambda i, te, tv: (layer, te[i], 0, 0))],
        out_specs=pl.BlockSpec((tm, d), lambda i, te, tv: (i, 0)),
        scratch_shapes=[pltpu.VMEM((d, n2), BF16), pltpu.VMEM((de, d), BF16)])
    return pl.pallas_call(
        _expert_kernel, name="moe_experts",
        grid_spec=grid_spec,
        out_shape=jax.ShapeDtypeStruct((rows, d), F32),
        compiler_params=_cp(("arbitrary",)),
    )(tile_expert, tile_valid, xs, w_gu, b_gu.reshape(depth, ne, 1, n2), w_down, b_down.reshape(depth, ne, 1, d))


def _combine_kernel(x_ref, og0_ref, og1_ref, og2_ref, og3_ref, rec_ref, gt_ref, fg_ref, x2_ref, *, final):
    rec = rec_ref[...]
    y = rec[:, :, ROUTE_WT:ROUTE_WT + 1] * og0_ref[...]
    for k, og_ref in ((1, og1_ref), (2, og2_ref), (3, og3_ref)):
        y = y + rec[:, :, ROUTE_WT + k:ROUTE_WT + k + 1] * og_ref[...]
    x2 = x_ref[...] + gt_ref[...] * y
    if final:
        x2 = x2 * lax.rsqrt(jnp.mean(x2 * x2, axis=-1, keepdims=True) + RMS_EPS) * fg_ref[...]
    x2_ref[...] = x2


def _combine(x1, og_all, tok_off, rec, mod, final_g, final, tokens=256):
    seq, batch, d = x1.shape
    lt, bb = _tiles(seq, batch, tokens)
    nl = rec.shape[-1]
    t_all = og_all.shape[0] // TOP_K
    step = batch * lt
    if t_all % step == 0 and tok_off % step == 0:
        og = og_all.reshape(TOP_K * t_all // batch, batch, d)
        starts = [(k * t_all + tok_off) // step for k in range(TOP_K)]
    else:
        og = jnp.concatenate([og_all[k * t_all + tok_off:k * t_all + tok_off + seq * batch] for k in range(TOP_K)],
                             axis=0).reshape(TOP_K * seq, batch, d)
        starts = [k * (seq // lt) for k in range(TOP_K)]
    ogspec = lambda k: pl.BlockSpec((lt, bb, d), lambda b, t: (starts[k] + t, b, 0))
    return pl.pallas_call(
        functools.partial(_combine_kernel, final=final), name="moe_combine",
        grid=(batch // bb, seq // lt),
        in_specs=[pl.BlockSpec((lt, bb, d), lambda b, t: (t, b, 0)),
                  ogspec(0), ogspec(1), ogspec(2), ogspec(3),
                  pl.BlockSpec((lt, bb, nl), lambda b, t: (t, b, 0)),
                  pl.BlockSpec((1, bb, d), lambda b, t: (0, b, 5)),
                  pl.BlockSpec((1, 1, d), lambda b, t: (0, 0, 0))],
        out_specs=pl.BlockSpec((lt, bb, d), lambda b, t: (t, b, 0)),
        out_shape=jax.ShapeDtypeStruct((seq, batch, d), F32),
        compiler_params=_cp(("arbitrary", "arbitrary")),
    )(x1, og, og, og, og, rec, mod, final_g.reshape(1, 1, d))


def _pad_in_cols(a, axis):
    d = D_MODEL
    main = lax.slice_in_dim(a, 0, 6 * d, axis=axis)
    lora = lax.slice_in_dim(a, 6 * d, 6 * d + RW_LORA, axis=axis)
    gates = lax.slice_in_dim(a, 6 * d + RW_LORA, 9 * d + RW_LORA, axis=axis)
    pad_shape = list(a.shape)
    pad_shape[axis] = LORA_PAD - RW_LORA
    return jnp.concatenate([main, gates, lora, jnp.zeros(pad_shape, a.dtype)], axis=axis)


def _mix_group(x, mod, st, lp, s5p):
    seq, batch, d = x.shape
    s5_re, s5_im, lru_h, lru_conv, rw_s, rw_shift = st
    z = _in_proj(x, mod, lp["norm1_g"], lp["w_in_p"])

    y_s5, n_s5_re, n_s5_im = _s5_mixer(
        z, s5p["ab_re"], s5p["ab_im"], s5p["bb_re"], s5p["bb_im"], s5p["ct_re"], s5p["ct_im"], s5p["d"],
        s5_re.reshape(batch, S5_LANES), s5_im.reshape(batch, S5_LANES))

    y_lru, n_lru_h, tail = _lru_mixer(
        z, lru_conv.transpose(1, 0, 2), lp["lru_conv_w"], lp["lru_conv_b"], lp["lru_w_a"], lp["lru_b_a"],
        lp["lru_w_x"], lp["lru_b_x"], lp["lru_lam"], lru_h)

    y_rw, n_rw_s = _rwkv(z, rw_shift, rw_s, lp)
    n_rw_shift = jnp.concatenate([z[seq - 1, :, COL_R:COL_R + 3 * d], z[seq - 1, :, COL_LORA:COL_LORA + RW_LORA]],
                                 axis=-1)

    x1, h2, logits = _merge(x, y_s5, y_lru, y_rw, z, mod, lp["s5_w_glu"], lp["s5_b_glu"], lp["w_br_s5"],
                            lp["w_br_lru"], lp["w_br_rw"], lp["w_out"], lp["norm2_g"], lp["wr_p"], lp["br_p"])
    new = (n_s5_re.reshape(batch, S5_GROUPS, S5_STATE), n_s5_im.reshape(batch, S5_GROUPS, S5_STATE),
           n_lru_h, tail.transpose(1, 0, 2), n_rw_s, n_rw_shift)
    return x1, h2, logits, new


def _moe(h2_all, logits_all, layer, moe_params):
    t, d = h2_all.shape
    tm = EXPERT_TILE
    rec, cnt = _route(logits_all)
    idx = rec[:, ROUTE_IDX:ROUTE_IDX + TOP_K].astype(jnp.int32)
    rank = rec[:, ROUTE_RANK:ROUTE_RANK + TOP_K].astype(jnp.int32)
    counts = cnt[0, :N_EXPERTS].astype(jnp.int32)
    padded = ((counts + tm - 1) // tm) * tm
    ends = jnp.cumsum(padded)
    offs = ends - padded
    pos = offs[idx] + rank
    n_tiles = (t * TOP_K + N_EXPERTS * (tm - 1) + tm - 1) // tm
    rows = n_tiles * tm
    src = jnp.zeros((rows,), jnp.int32).at[pos.reshape(-1)].set(
        jnp.repeat(jnp.arange(t, dtype=jnp.int32), TOP_K), unique_indices=True, mode="promise_in_bounds")
    starts = jnp.arange(n_tiles, dtype=jnp.int32) * tm
    tile_valid = (starts < ends[-1]).astype(jnp.int32)
    owner = lambda row: jnp.sum((ends[None, :] <= row[:, None]).astype(jnp.int32), axis=1)
    last = owner(ends[-1:] - 1)[0]
    tile_expert = jnp.clip(jnp.where(tile_valid > 0, owner(starts), last), 0, N_EXPERTS - 1)
    xs = h2_all.at[src].get(mode="promise_in_bounds")
    os_ = _experts(tile_expert, tile_valid, xs, layer, *moe_params)
    og = os_.at[pos.T.reshape(-1)].get(mode="promise_in_bounds")
    return og, rec


def kernel(x_prompt, x_sample, state_s5_re, state_s5_im, state_lru_h, cache_lru_conv, state_rwkv, cache_rwkv_shift, c_prompt, c_sample, w_mod, b_mod, norm1_g, w_in, s5_lam_re, s5_lam_im, s5_log_step, s5_b_re, s5_b_im, s5_c_re, s5_c_im, s5_d, s5_w_glu, s5_b_glu, lru_conv_w, lru_conv_b, lru_w_a, lru_b_a, lru_w_x, lru_b_x, lru_lam, rw_mu, rw_w0, rw_w2, rw_a0, rw_a2, rw_g2, rw_k_k, rw_k_a, rw_r_k, rw_ln_w, rw_ln_b, w_br_s5, w_br_lru, w_br_rw, w_out, norm2_g, moe_w_router, moe_b_router, moe_w_gu, moe_b_gu, moe_w_down, moe_b_down, final_g):
    depth = w_mod.shape[0]
    d = D_MODEL
    bp, lp_len = x_prompt.shape[0], x_prompt.shape[1]
    bs, ls_len = x_sample.shape[0], x_sample.shape[1]

    mod_all = _modulation(jnp.concatenate([c_prompt, c_sample], axis=0), w_mod, b_mod)

    ab_re, ab_im, q_re, q_im = _s5_params(s5_lam_re, s5_lam_im, s5_log_step)
    shp = (depth, S5_GROUPS, S5_STATE)
    q_re, q_im = q_re.reshape(shp)[..., None], q_im.reshape(shp)[..., None]
    bbar_re = q_re * s5_b_re - q_im * s5_b_im
    bbar_im = q_re * s5_b_im + q_im * s5_b_re
    per = LANES // S5_GROUP

    xs = [_swap_leading(x_prompt), _swap_leading(x_sample)]
    zeros_like_state = lambda s, b: jnp.zeros((b,) + s.shape[2:], F32)
    sample_states = (state_s5_re, state_s5_im, state_lru_h, cache_lru_conv, state_rwkv, cache_rwkv_shift)
    collected = [tuple([] for _ in sample_states), tuple([] for _ in sample_states)]

    for l in range(depth):
        lp = dict(
            norm1_g=norm1_g[l], w_in_p=_pad_in_cols(w_in[l], 1).astype(BF16),
            lru_conv_w=lru_conv_w[l], lru_conv_b=lru_conv_b[l], lru_w_a=lru_w_a[l], lru_b_a=lru_b_a[l],
            lru_w_x=lru_w_x[l], lru_b_x=lru_b_x[l], lru_lam=lru_lam[l],
            rw_mu=rw_mu[l], rw_w0=rw_w0[l], rw_a0=rw_a0[l],
            rw_w2p=jnp.concatenate([rw_w2[l], jnp.zeros((LANES - RW_W_LORA, d), F32)], axis=0).astype(BF16),
            rw_a2p=jnp.concatenate([jnp.zeros((RW_W_LORA, d), F32), rw_a2[l]], axis=0).astype(BF16),
            rw_g2p=jnp.concatenate([rw_g2[l], jnp.zeros((2 * LANES - RW_G_LORA, d), F32)], axis=0).astype(BF16),
            rw_k_k=rw_k_k[l], rw_k_a=rw_k_a[l], rw_r_k=rw_r_k[l], rw_ln_w=rw_ln_w[l], rw_ln_b=rw_ln_b[l],
            s5_w_glu=s5_w_glu[l].astype(BF16), s5_b_glu=s5_b_glu[l], w_br_s5=w_br_s5[l].astype(BF16),
            w_br_lru=w_br_lru[l].astype(BF16), w_br_rw=w_br_rw[l].astype(BF16), w_out=w_out[l].astype(BF16),
            norm2_g=norm2_g[l],
            wr_p=jnp.concatenate([moe_w_router[l], jnp.zeros((d, LANES - N_EXPERTS), F32)], axis=1),
            br_p=jnp.concatenate([moe_b_router[l], jnp.zeros((LANES - N_EXPERTS,), F32)]).reshape(1, LANES))
        g0 = l * S5_GROUPS
        s5p = dict(
            ab_re=ab_re[g0:g0 + S5_GROUPS].reshape(1, S5_LANES), ab_im=ab_im[g0:g0 + S5_GROUPS].reshape(1, S5_LANES),
            bb_re=_block_diag(bbar_re[l].transpose(0, 2, 1), per).astype(BF16),
            bb_im=_block_diag(bbar_im[l].transpose(0, 2, 1), per).astype(BF16),
            ct_re=_block_diag(s5_c_re[l].transpose(0, 2, 1), per).astype(BF16),
            ct_im=_block_diag(s5_c_im[l].transpose(0, 2, 1), per).astype(BF16),
            d=s5_d[l].reshape(1, d))

        mods = [mod_all[l, :bp][None], mod_all[l, bp:][None]]
        states = [tuple(zeros_like_state(s, bp) for s in sample_states), tuple(s[l] for s in sample_states)]
        x1s, h2s, lgs = [], [], []
        for gi in range(2):
            x1, h2, lg, new = _mix_group(xs[gi], mods[gi], states[gi], lp, s5p)
            x1s.append(x1)
            h2s.append(h2.reshape(-1, d))
            lgs.append(lg.reshape(-1, LANES))
            for lst, s in zip(collected[gi], new):
                lst.append(s)

        og, rec = _moe(jnp.concatenate(h2s[::-1], axis=0), jnp.concatenate(lgs[::-1], axis=0), l,
                       (moe_w_gu, moe_b_gu, moe_w_down, moe_b_down))
        ts = ls_len * bs
        offs = [ts, 0]
        recs = [rec[ts:].reshape(lp_len, bp, LANES), rec[:ts].reshape(ls_len, bs, LANES)]
        xs = [_combine(x1s[gi], og, offs[gi], recs[gi], mods[gi], final_g, final=(l == depth - 1))
              for gi in range(2)]

    y_prompt = _swap_leading(xs[0])
    y_sample = _swap_leading(xs[1])
    p_states = tuple(jnp.stack(lst) for lst in collected[0])
    s_states = tuple(jnp.stack(lst) for lst in collected[1])
    return (y_prompt, y_sample) + p_states + s_states
```

```python
import functools
import math

import jax
import jax.numpy as jnp
from jax import lax
from jax.experimental import pallas as pl
from jax.experimental.pallas import tpu as pltpu

F32 = jnp.float32
BF16 = jnp.bfloat16
HIGHEST = lax.Precision.HIGHEST

D_MODEL = 1024
RMS_EPS = 1e-5
S5_GROUP = 16
S5_GROUPS = D_MODEL // S5_GROUP
S5_STATE = 64
S5_LANES = S5_GROUPS * S5_STATE
LRU_HEADS = 8
LRU_BLOCK = D_MODEL // LRU_HEADS
CONV_WIDTH = 4
LRU_C = 8.0
RW_HEAD = 64
RW_HEADS = D_MODEL // RW_HEAD
RW_W_LORA = 64
RW_A_LORA = 64
RW_G_LORA = 160
RW_LORA = RW_W_LORA + RW_A_LORA + RW_G_LORA
RW_COLS = 3 * D_MODEL + RW_LORA
RW_LN_EPS = 64e-5
N_EXPERTS = 32
TOP_K = 4
SWIGLU_ALPHA = 1.702
SWIGLU_LIMIT = 7.0

LANES = 128
SUBLANES = 8
VMEM_LIMIT = 56 * 1024 * 1024

LORA_PAD = 512
COL_S5 = 0
COL_XLRU = 1 * D_MODEL
COL_GLRU = 2 * D_MODEL
COL_R = 3 * D_MODEL
COL_K = 4 * D_MODEL
COL_V = 5 * D_MODEL
COL_GATE = 6 * D_MODEL
COL_LORA = 9 * D_MODEL
D_IN_PAD = COL_LORA + LORA_PAD

S5_LANE_BLOCK = LANES
S5_STATE_BLOCK = (LANES // S5_GROUP) * S5_STATE
EXPERT_TILE = 512
ROUTE_TILE = 512


def _cp(sem, vmem=VMEM_LIMIT):
    return pltpu.CompilerParams(dimension_semantics=sem, vmem_limit_bytes=vmem)


def _gelu(x):
    return 0.5 * x * (1.0 + jnp.tanh(math.sqrt(2.0 / math.pi) * (x + 0.044715 * (x * x * x))))


def _sigmoid(x):
    return 1.0 / (1.0 + jnp.exp(-x))


def _softplus(x):
    return jnp.maximum(x, 0.0) + jnp.log1p(jnp.exp(-jnp.abs(x)))


def _tiles(seq, batch, tokens):
    bb = min(batch, max(SUBLANES, (tokens // seq) // SUBLANES * SUBLANES))
    while batch % bb:
        bb -= SUBLANES
    lt = max(1, min(seq, tokens // bb))
    while seq % lt:
        lt -= 1
    return lt, bb


def _swap_kernel(x_ref, o_ref):
    n0, n1, _ = x_ref.shape
    if n0 <= n1:
        for i in range(n0):
            o_ref[:, i, :] = x_ref[i]
    else:
        for i in range(n1):
            o_ref[i] = x_ref[:, i, :]


def _swap_leading(x, tokens=1024):
    a, b, d = x.shape
    ta = min(a, SUBLANES) if a <= b else min(a, max(SUBLANES, tokens // b))
    tb = min(b, max(SUBLANES, tokens // ta))
    while a % ta:
        ta -= SUBLANES
    while b % tb:
        tb -= SUBLANES
    return pl.pallas_call(
        _swap_kernel, name="swap_leading",
        grid=(a // ta, b // tb),
        in_specs=[pl.BlockSpec((ta, tb, d), lambda i, j: (i, j, 0))],
        out_specs=pl.BlockSpec((tb, ta, d), lambda i, j: (j, i, 0)),
        out_shape=jax.ShapeDtypeStruct((b, a, d), x.dtype),
        compiler_params=_cp(("arbitrary", "arbitrary")),
    )(x)


def _mod_kernel(c_ref, w_ref, b_ref, o_ref):
    c = c_ref[...]
    s = c * _sigmoid(c)
    o_ref[0] = jnp.dot(s, w_ref[0], precision=HIGHEST, preferred_element_type=F32) + b_ref[0]


def _modulation(c, w_mod, b_mod):
    depth, d, n = w_mod.shape
    bc = c.shape[0]
    tn = 1536
    return pl.pallas_call(
        _mod_kernel, name="adaln_mod",
        grid=(depth, n // tn),
        in_specs=[pl.BlockSpec((bc, d), lambda l, j: (0, 0)),
                  pl.BlockSpec((1, d, tn), lambda l, j: (l, 0, j)),
                  pl.BlockSpec((1, 1, tn), lambda l, j: (l, 0, j))],
        out_specs=pl.BlockSpec((1, bc, tn), lambda l, j: (l, 0, j)),
        out_shape=jax.ShapeDtypeStruct((depth, bc, n), F32),
        compiler_params=_cp(("arbitrary", "arbitrary")),
    )(c, w_mod, b_mod.reshape(depth, 1, n))


def _in_kernel(x_ref, sh_ref, sc_ref, g_ref, w_ref, o_ref, h_scr):
    lt, bb, d = x_ref.shape

    @pl.when(pl.program_id(2) == 0)
    def _():
        x = x_ref[...]
        y = x * lax.rsqrt(jnp.mean(x * x, axis=-1, keepdims=True) + RMS_EPS) * g_ref[...]
        h = y * (1.0 + sc_ref[...]) + sh_ref[...]
        h_scr[...] = h.reshape(lt * bb, d).astype(BF16)

    o = jnp.dot(h_scr[...], w_ref[...], preferred_element_type=F32)
    o_ref[...] = o.reshape(lt, bb, o.shape[-1])


def _in_proj(x, mod, norm_g, w_in_p, tokens=2048, tn=512):
    seq, batch, d = x.shape
    lt, bb = _tiles(seq, batch, tokens)
    n = w_in_p.shape[1]
    return pl.pallas_call(
        _in_kernel, name="in_proj",
        grid=(batch // bb, seq // lt, n // tn),
        in_specs=[pl.BlockSpec((lt, bb, d), lambda b, t, j: (t, b, 0)),
                  pl.BlockSpec((1, bb, d), lambda b, t, j: (0, b, 0)),
                  pl.BlockSpec((1, bb, d), lambda b, t, j: (0, b, 1)),
                  pl.BlockSpec((1, 1, d), lambda b, t, j: (0, 0, 0)),
                  pl.BlockSpec((d, tn), lambda b, t, j: (0, j))],
        out_specs=pl.BlockSpec((lt, bb, tn), lambda b, t, j: (t, b, j)),
        out_shape=jax.ShapeDtypeStruct((seq, batch, n), F32),
        scratch_shapes=[pltpu.VMEM((lt * bb, d), BF16)],
        compiler_params=_cp(("arbitrary", "arbitrary", "arbitrary")),
    )(x, mod, mod, norm_g.reshape(1, 1, d), w_in_p)


def _s5_kernel(u_ref, bbr_ref, bbi_ref, ar_ref, ai_ref, ctr_ref, cti_ref, d_ref, h0r_ref, h0i_ref,
               y_ref, hr_ref, hi_ref, sr, si):
    lt, bb, nl = u_ref.shape
    ns = sr.shape[-1]

    @pl.when(pl.program_id(2) == 0)
    def _():
        hr_ref[...] = h0r_ref[...]
        hi_ref[...] = h0i_ref[...]

    u2 = u_ref[...].reshape(lt * bb, nl)
    ub = u2.astype(BF16)
    sr[...] = jnp.dot(ub, bbr_ref[0], preferred_element_type=F32).reshape(lt, bb, ns)
    si[...] = jnp.dot(ub, bbi_ref[0], preferred_element_type=F32).reshape(lt, bb, ns)
    ar = jnp.broadcast_to(ar_ref[...], (bb, ns))
    ai = jnp.broadcast_to(ai_ref[...], (bb, ns))

    def body(i, carry):
        hr, hi = carry
        nr = ar * hr - ai * hi + sr[i]
        ni = ar * hi + ai * hr + si[i]
        sr[i] = nr
        si[i] = ni
        return nr, ni

    hr, hi = lax.fori_loop(0, lt, body, (hr_ref[...], hi_ref[...]), unroll=min(lt, 8))
    hr_ref[...] = hr
    hi_ref[...] = hi
    y = (jnp.dot(sr[...].reshape(lt * bb, ns).astype(BF16), ctr_ref[0], preferred_element_type=F32)
         - jnp.dot(si[...].reshape(lt * bb, ns).astype(BF16), cti_ref[0], preferred_element_type=F32)
         + d_ref[...] * u2)
    y_ref[...] = y.reshape(lt, bb, nl)


def _s5_params_kernel(lr_ref, li_ref, ls_ref, abr_ref, abi_ref, qr_ref, qi_ref):
    lr = lr_ref[...]
    li = li_ref[...]
    step = jnp.exp(ls_ref[...])
    mag = jnp.exp(lr * step)
    ab_re = mag * jnp.cos(li * step)
    ab_im = mag * jnp.sin(li * step)
    den = lr * lr + li * li
    abr_ref[...] = ab_re
    abi_ref[...] = ab_im
    qr_ref[...] = ((ab_re - 1.0) * lr + ab_im * li) / den
    qi_ref[...] = (ab_im * lr - (ab_re - 1.0) * li) / den


def _s5_params(lam_re, lam_im, log_step):
    depth, g, n = lam_re.shape
    shp = jax.ShapeDtypeStruct((depth * g, n), F32)
    return pl.pallas_call(_s5_params_kernel, name="s5_params", out_shape=(shp, shp, shp, shp))(
        lam_re.reshape(depth * g, n), lam_im.reshape(depth * g, n), log_step.reshape(depth * g, 1))


def _block_diag(blocks, per):
    g, a, b = blocks.shape
    x = blocks.reshape(g // per, per, a, b)
    eye = jnp.eye(per, dtype=blocks.dtype)
    return jnp.einsum("gpab,pq->gpaqb", x, eye).reshape(g // per, per * a, per * b)


def _s5_mixer(z, ab_re, ab_im, bb_re, bb_im, ct_re, ct_im, d_skip, h0_re, h0_im, tokens=512):
    seq, batch, _ = z.shape
    lt, bb = _tiles(seq, batch, tokens)
    nl, ns = S5_LANE_BLOCK, S5_STATE_BLOCK
    nblk = D_MODEL // nl
    col0 = COL_S5 // nl
    y, hr, hi = pl.pallas_call(
        _s5_kernel, name="s5_mixer",
        grid=(batch // bb, nblk, seq // lt),
        in_specs=[pl.BlockSpec((lt, bb, nl), lambda b, j, t: (t, b, col0 + j)),
                  pl.BlockSpec((1, nl, ns), lambda b, j, t: (j, 0, 0)),
                  pl.BlockSpec((1, nl, ns), lambda b, j, t: (j, 0, 0)),
                  pl.BlockSpec((1, ns), lambda b, j, t: (0, j)),
                  pl.BlockSpec((1, ns), lambda b, j, t: (0, j)),
                  pl.BlockSpec((1, ns, nl), lambda b, j, t: (j, 0, 0)),
                  pl.BlockSpec((1, ns, nl), lambda b, j, t: (j, 0, 0)),
                  pl.BlockSpec((1, nl), lambda b, j, t: (0, j)),
                  pl.BlockSpec((bb, ns), lambda b, j, t: (b, j)),
                  pl.BlockSpec((bb, ns), lambda b, j, t: (b, j))],
        out_specs=[pl.BlockSpec((lt, bb, nl), lambda b, j, t: (t, b, j)),
                   pl.BlockSpec((bb, ns), lambda b, j, t: (b, j)),
                   pl.BlockSpec((bb, ns), lambda b, j, t: (b, j))],
        out_shape=(jax.ShapeDtypeStruct((seq, batch, D_MODEL), F32),
                   jax.ShapeDtypeStruct((batch, S5_LANES), F32),
                   jax.ShapeDtypeStruct((batch, S5_LANES), F32)),
        scratch_shapes=[pltpu.VMEM((lt, bb, ns), F32), pltpu.VMEM((lt, bb, ns), F32)],
        compiler_params=_cp(("arbitrary", "arbitrary", "arbitrary")),
    )(z, bb_re, bb_im, ab_re, ab_im, ct_re, ct_im, d_skip, h0_re, h0_im)
    return y, hr, hi


def _lru_kernel(x_ref, g_ref, cbuf_ref, cw_ref, cb_ref, wa_ref, ba_ref, wx_ref, bx_ref, lam_ref, h0_ref,
                y_ref, hn_ref, tail_ref, prev_s, a_s, h_s):
    lt, bb, w = x_ref.shape
    taps = cw_ref.shape[0]

    @pl.when(pl.program_id(2) == 0)
    def _():
        hn_ref[...] = h0_ref[...]
        prev_s[...] = cbuf_ref[...]

    xp = jnp.concatenate([prev_s[...], x_ref[...]], axis=0)
    xc = cb_ref[...] + xp[0:lt] * cw_ref[0]
    for tap in range(1, taps):
        xc = xc + xp[tap:tap + lt] * cw_ref[tap]
    prev_s[...] = xp[lt:lt + taps - 1]
    tail_ref[...] = xp[lt:lt + taps - 1]

    xc2 = xc.reshape(lt * bb, w)
    xb = xc2.astype(BF16)
    r = _sigmoid(jnp.dot(xb, wa_ref[0], preferred_element_type=F32) + ba_ref[...])
    i = _sigmoid(jnp.dot(xb, wx_ref[0], preferred_element_type=F32) + bx_ref[...])
    log_a = (-LRU_C) * r * _softplus(-lam_ref[...])
    a = jnp.exp(log_a)
    th = jnp.tanh(log_a)
    neg_expm1 = -2.0 * th / (1.0 - th)
    b = jnp.sqrt(neg_expm1) * (i * xc2)
    a_s[...] = a.reshape(lt, bb, w)
    h_s[...] = b.reshape(lt, bb, w)

    def body(t, h):
        h = a_s[t] * h + h_s[t]
        h_s[t] = h
        return h

    hn_ref[...] = lax.fori_loop(0, lt, body, hn_ref[...], unroll=min(lt, 8))
    y_ref[...] = h_s[...] * _gelu(g_ref[...])


def _lru_mixer(z, conv_buf_t, conv_w, conv_b, w_a, b_a, w_x, b_x, lam, h0, tokens=1024):
    seq, batch, _ = z.shape
    lt, bb = _tiles(seq, batch, tokens)
    w = LRU_BLOCK
    xcol, gcol = COL_XLRU // w, COL_GLRU // w
    taps = CONV_WIDTH
    vec = lambda a: a.reshape(1, D_MODEL)
    vspec = pl.BlockSpec((1, w), lambda b, h, t: (0, h))
    y, hn, tail = pl.pallas_call(
        _lru_kernel, name="rglru_mixer",
        grid=(batch // bb, LRU_HEADS, seq // lt),
        in_specs=[pl.BlockSpec((lt, bb, w), lambda b, h, t: (t, b, xcol + h)),
                  pl.BlockSpec((lt, bb, w), lambda b, h, t: (t, b, gcol + h)),
                  pl.BlockSpec((taps - 1, bb, w), lambda b, h, t: (0, b, h)),
                  pl.BlockSpec((taps, 1, w), lambda b, h, t: (0, 0, h)),
                  vspec,
                  pl.BlockSpec((1, w, w), lambda b, h, t: (h, 0, 0)),
                  vspec,
                  pl.BlockSpec((1, w, w), lambda b, h, t: (h, 0, 0)),
                  vspec, vspec,
                  pl.BlockSpec((bb, w), lambda b, h, t: (b, h))],
        out_specs=[pl.BlockSpec((lt, bb, w), lambda b, h, t: (t, b, h)),
                   pl.BlockSpec((bb, w), lambda b, h, t: (b, h)),
                   pl.BlockSpec((taps - 1, bb, w), lambda b, h, t: (0, b, h))],
        out_shape=(jax.ShapeDtypeStruct((seq, batch, D_MODEL), F32),
                   jax.ShapeDtypeStruct((batch, D_MODEL), F32),
                   jax.ShapeDtypeStruct((taps - 1, batch, D_MODEL), F32)),
        scratch_shapes=[pltpu.VMEM((taps - 1, bb, w), F32), pltpu.VMEM((lt, bb, w), F32),
                        pltpu.VMEM((lt, bb, w), F32)],
        compiler_params=_cp(("arbitrary", "arbitrary", "arbitrary")),
    )(z, z, conv_buf_t, conv_w.reshape(taps, 1, D_MODEL), vec(conv_b), w_a.astype(BF16), vec(b_a),
      w_x.astype(BF16), vec(b_x), vec(lam), h0)
    return y, hn, tail


def _rwkv_kernel(zr_ref, zk_ref, zv_ref, zl_ref, sh_ref, shl_ref, mu_ref, mul_ref, w0_ref, w2_ref, a0_ref, a2_ref,
                 g2_ref, kk_ref, ka_ref, rk_ref, lnw_ref, lnb_ref, s0_ref, sbuf_ref,
                 y_ref, sn_ref,
                 prev_s, prevl_s, s_scr, sa_s, nkk_s, r_s, w_s, k_s, v_s, b_s, y_s):
    lt, bb, wd = zr_ref.shape
    kp = wd // LANES
    hd = RW_HEAD
    nl = zl_ref.shape[-1]
    n = lt * bb
    ti = pl.program_id(2)

    @pl.when(ti == 0)
    def _():
        prev_s[...] = sh_ref[...]
        prevl_s[...] = shl_ref[...]
        for i in range(hd):
            rows = [s0_ref[0, :, 2 * c + par, i, :] for par in range(2) for c in range(kp)]
            s_scr[:, i, :] = jnp.concatenate(rows, axis=0).T

    def shifted(z_ref, prev, mu):
        z = z_ref[...]
        zp = jnp.concatenate([prev, z[0:lt - 1]], axis=0) if lt > 1 else prev
        return z + (zp - z) * mu, z[lt - 1:lt]

    r, last_r = shifted(zr_ref, prev_s[0:1], mu_ref[0:1])
    k, last_k = shifted(zk_ref, prev_s[1:2], mu_ref[1:2])
    v, last_v = shifted(zv_ref, prev_s[2:3], mu_ref[2:3])
    lo, last_l = shifted(zl_ref, prevl_s[...], mul_ref[...])
    prev_s[0:1] = last_r
    prev_s[1:2] = last_k
    prev_s[2:3] = last_v
    prevl_s[...] = last_l

    lo2 = lo.reshape(n, nl)
    wa_in = lo2[:, 0:LANES]
    g_in = lo2[:, LANES:3 * LANES]
    wpre = w0_ref[...] + jnp.dot(jnp.tanh(wa_in).astype(BF16), w2_ref[...], preferred_element_type=F32)
    decay = jnp.exp(-jnp.exp(-_softplus(-wpre) - 0.5))
    a = _sigmoid(a0_ref[...] + jnp.dot(wa_in.astype(BF16), a2_ref[...], preferred_element_type=F32))
    g = jnp.dot(_sigmoid(g_in).astype(BF16), g2_ref[...], preferred_element_type=F32)
    k2 = k.reshape(n, wd)
    shp = (lt, bb, wd)

    def to_lanes(q):
        rows = jnp.concatenate([q[:, :, c * LANES:(c + 1) * LANES] for c in range(kp)], axis=1) if kp > 1 else q
        t = jnp.swapaxes(rows, 1, 2)
        return jnp.concatenate([t[:, 0:hd, :], t[:, hd:2 * hd, :]], axis=2)

    r_s[...] = to_lanes(r)
    w_s[...] = to_lanes(decay.reshape(shp))
    k_s[...] = to_lanes((k2 * (1.0 + (a - 1.0) * ka_ref[...])).reshape(shp))
    v_s[...] = to_lanes(v)
    kk = to_lanes((k2 * kk_ref[...]).reshape(shp))
    kkn = kk * lax.rsqrt(jnp.maximum(jnp.sum(kk * kk, axis=1, keepdims=True), 1e-24))
    nkk_s[0:lt] = -kkn
    nkk_s[lt:lt + 1] = jnp.zeros((1, hd, LANES), F32)
    b_s[...] = kkn * to_lanes(a.reshape(shp))

    acc0 = [jnp.zeros((hd, LANES), F32) for _ in range(2)]
    for j in range(hd):
        acc0[j % 2] = acc0[j % 2] + s_scr[j] * nkk_s[0, pl.ds(j, 1), :]
    sa_s[...] = acc0[0] + acc0[1]
    rg = hd // 2

    def step(t, carry):
        row = lambda ref, tt, j: ref[tt, pl.ds(j, 1), :]

        def group(g, c2):
            rows = pl.ds(pl.multiple_of(g * rg, rg), rg)
            sa = sa_s[rows, :]
            vt = v_s[t, rows, :]
            nac = [jnp.zeros((rg, LANES), F32) for _ in range(2)]
            yac = [jnp.zeros((rg, LANES), F32) for _ in range(2)]
            for j in range(hd):
                sj = s_scr[j, rows, :] * row(w_s, t, j) + sa * row(b_s, t, j) + vt * row(k_s, t, j)
                s_scr[j, rows, :] = sj
                yac[j % 2] = yac[j % 2] + sj * row(r_s, t, j)
                nac[j % 2] = nac[j % 2] + sj * row(nkk_s, t + 1, j)
            y_s[t, rows, :] = yac[0] + yac[1]
            sa_s[rows, :] = nac[0] + nac[1]
            return c2

        lax.fori_loop(0, hd // rg, group, 0)
        return carry

    lax.fori_loop(0, lt, step, 0)

    ys = y_s[...]
    mean = jnp.mean(ys, axis=1, keepdims=True)
    yc = ys - mean
    var = jnp.mean(yc * yc, axis=1, keepdims=True)
    yn = yc * lax.rsqrt(var + RW_LN_EPS)
    bonus = jnp.sum(r_s[...] * k_s[...] * rk_ref[...], axis=1, keepdims=True)
    o = yn * lnw_ref[...] + lnb_ref[...] + bonus * v_s[...]
    o = jnp.swapaxes(jnp.concatenate([o[:, :, 0:hd], o[:, :, hd:2 * hd]], axis=1), 1, 2)
    nat = jnp.concatenate([o[:, c * bb:(c + 1) * bb, :] for c in range(kp)], axis=2) if kp > 1 else o
    y_ref[...] = nat * g.reshape(shp)

    @pl.when(ti == pl.num_programs(2) - 1)
    def _():
        for i in range(hd):
            m = s_scr[:, i, :].T
            for par in range(2):
                for c in range(kp):
                    r0 = (par * kp + c) * bb
                    sn_ref[0, :, 2 * c + par, i, :] = m[r0:r0 + bb, :]


def _rwkv(z, rw_shift, s_in, s_layer, s_out, layer, lp, lt=32):
    seq, batch, _ = z.shape
    d, hd, nl = D_MODEL, RW_HEAD, LORA_PAD
    bb, kp = _rw_tiles(batch)
    wd = LANES * kp
    n_pg = d // wd
    lt = min(lt, seq)
    while seq % lt:
        lt -= 1
    pad = lambda a: jnp.concatenate([a, jnp.zeros(a.shape[:-1] + (nl - RW_LORA,), a.dtype)], axis=-1)
    sh = rw_shift[:, 0:3 * d].reshape(batch, 3, d).transpose(1, 0, 2)
    shl = pad(rw_shift[:, 3 * d:])[None]
    mu = lp["rw_mu"][0:3 * d].reshape(3, 1, d)
    mul = pad(lp["rw_mu"][3 * d:]).reshape(1, 1, nl)
    vec = lambda a: a.reshape(1, d)
    zspec = lambda col: pl.BlockSpec((lt, bb, wd), lambda b, g, t: (t, b, col // wd + g))
    vspec = pl.BlockSpec((1, wd), lambda b, g, t: (0, g))
    wspec = lambda rows: pl.BlockSpec((rows, wd), lambda b, g, t: (0, g))
    pspec = pl.BlockSpec((1, hd, LANES), lambda b, g, t: (g, 0, 0))
    sspec = lambda lyr: pl.BlockSpec((1, bb, 2 * kp, hd, hd), lambda b, g, t: (lyr, b, g, 0, 0))
    chunk = pltpu.VMEM((lt, hd, LANES), F32)
    y, sn = pl.pallas_call(
        _rwkv_kernel, name="rwkv7_mixer",
        grid=(batch // bb, n_pg, seq // lt),
        in_specs=[zspec(COL_R), zspec(COL_K), zspec(COL_V),
                  pl.BlockSpec((lt, bb, nl), lambda b, g, t: (t, b, COL_LORA // nl)),
                  pl.BlockSpec((3, bb, wd), lambda b, g, t: (0, b, g)),
                  pl.BlockSpec((1, bb, nl), lambda b, g, t: (0, b, 0)),
                  pl.BlockSpec((3, 1, wd), lambda b, g, t: (0, 0, g)),
                  pl.BlockSpec((1, 1, nl), lambda b, g, t: (0, 0, 0)),
                  vspec, wspec(LANES), vspec, wspec(LANES), wspec(2 * LANES), vspec, vspec,
                  pspec, pspec, pspec, sspec(s_layer), pl.BlockSpec(memory_space=pl.ANY)],
        out_specs=[pl.BlockSpec((lt, bb, wd), lambda b, g, t: (t, b, g)), sspec(layer)],
        out_shape=(jax.ShapeDtypeStruct((seq, batch, d), F32),
                   jax.ShapeDtypeStruct(s_out.shape, F32)),
        input_output_aliases={19: 1},
        scratch_shapes=[pltpu.VMEM((3, bb, wd), F32), pltpu.VMEM((1, bb, nl), F32),
                        pltpu.VMEM((hd, hd, LANES), F32), pltpu.VMEM((hd, LANES), F32),
                        pltpu.VMEM((lt + 1, hd, LANES), F32)] + [chunk] * 6,
        compiler_params=_cp(("arbitrary", "arbitrary", "arbitrary")),
    )(z, z, z, z, sh, shl, mu, mul, vec(lp["rw_w0"]), lp["rw_w2p"], vec(lp["rw_a0"]), lp["rw_a2p"], lp["rw_g2p"],
      vec(lp["rw_k_k"]), vec(lp["rw_k_a"]), _rw_param_blocks(lp["rw_r_k"].reshape(-1), bb, kp),
      _rw_param_blocks(lp["rw_ln_w"], bb, kp), _rw_param_blocks(lp["rw_ln_b"], bb, kp), s_in, s_out)
    return y, sn


def _rw_tiles(batch):
    bb = min(batch, LANES // 2)
    assert (LANES // 2) % bb == 0 and batch % bb == 0 and bb % SUBLANES == 0, batch
    return bb, (LANES // 2) // bb


def _rw_param_blocks(p, bb, kp):
    n_pg = RW_HEADS // (2 * kp)
    x = p.reshape(n_pg, kp, 2, RW_HEAD).transpose(0, 3, 2, 1)
    return jnp.broadcast_to(x[..., None], (n_pg, RW_HEAD, 2, kp, bb)).reshape(n_pg, RW_HEAD, LANES)


def _merge_kernel(x_ref, ys5_ref, ylru_ref, yrw_ref, zg1_ref, zg2_ref, zg3_ref,
                  gt1_ref, sc2_ref, sh2_ref, wglu_ref, bglu_ref, wb1_ref, wb2_ref, wb3_ref, wout_ref,
                  n2g_ref, wr_ref, br_ref,
                  x1_ref, h2_ref, lg_ref):
    lt, bb, d = x_ref.shape
    n = lt * bb
    mm = lambda a, w_ref: jnp.dot(a.astype(BF16), w_ref[...], preferred_element_type=F32)
    flat = lambda ref: ref[...].reshape(n, d)

    y1 = _gelu(flat(ys5_ref))
    y1 = y1 * _sigmoid(mm(y1, wglu_ref) + bglu_ref[...])
    y3 = flat(yrw_ref)
    merged = (_sigmoid(flat(zg1_ref)) * mm(y1, wb1_ref)
              + _sigmoid(flat(zg2_ref)) * mm(flat(ylru_ref), wb2_ref)
              + _sigmoid(flat(zg3_ref)) * mm(y3, wb3_ref))
    upd = mm(merged, wout_ref).reshape(lt, bb, d)
    x1 = x_ref[...] + gt1_ref[...] * upd
    x1_ref[...] = x1
    y = x1 * lax.rsqrt(jnp.mean(x1 * x1, axis=-1, keepdims=True) + RMS_EPS) * n2g_ref[...]
    h2 = y * (1.0 + sc2_ref[...]) + sh2_ref[...]
    h2_ref[...] = h2
    logits = jnp.dot(h2.reshape(n, d), wr_ref[...], precision=HIGHEST, preferred_element_type=F32)
    lg_ref[...] = (logits + br_ref[...]).reshape(lt, bb, lg_ref.shape[-1])


def _merge(x, y_s5, y_lru, y_rw, z, mod, w_glu, b_glu, wb1, wb2, wb3, w_out, norm2_g, wr_p, br_p,
           tokens=512):
    seq, batch, d = x.shape
    lt, bb = _tiles(seq, batch, tokens)
    act = pl.BlockSpec((lt, bb, d), lambda b, t: (t, b, 0))
    gate =lambda i: pl.BlockSpec((lt, bb, d), lambda b, t: (t, b, COL_GATE // d + i))
    modspec = lambda i: pl.BlockSpec((1, bb, d), lambda b, t: (0, b, i))
    wspec = pl.BlockSpec((d, d), lambda b, t: (0, 0), pipeline_mode=pl.Buffered(1))
    vspec = pl.BlockSpec((1, d), lambda b, t: (0, 0))
    ne = wr_p.shape[1]
    return pl.pallas_call(
        _merge_kernel, name="merge_norm2_router",
        grid=(batch // bb, seq // lt),
        in_specs=[act] * 4 + [gate(0), gate(1), gate(2), modspec(2), modspec(4), modspec(3),
                              wspec, vspec, wspec, wspec, wspec, wspec,
                              pl.BlockSpec((1, 1, d), lambda b, t: (0, 0, 0)),
                              pl.BlockSpec((d, ne), lambda b, t: (0, 0)),
                              pl.BlockSpec((1, ne), lambda b, t: (0, 0))],
        out_specs=[act, act, pl.BlockSpec((lt, bb, ne), lambda b, t: (t, b, 0))],
        out_shape=(jax.ShapeDtypeStruct((seq, batch, d), F32),
                   jax.ShapeDtypeStruct((seq, batch, d), F32),
                   jax.ShapeDtypeStruct((seq, batch, ne), F32)),
        compiler_params=_cp(("arbitrary", "arbitrary")),
    )(x, y_s5, y_lru, y_rw, z, z, z, mod, mod, mod, w_glu, b_glu.reshape(1, d), wb1, wb2, wb3, w_out,
      norm2_g.reshape(1, 1, d), wr_p, br_p)


ROUTE_IDX, ROUTE_RANK, ROUTE_WT = 0, TOP_K, 2 * TOP_K


def _route_kernel(lg_ref, rec_ref, cnt_ref, run_s):
    tt, nl = lg_ref.shape

    @pl.when(pl.program_id(0) == 0)
    def _():
        run_s[...] = jnp.zeros_like(run_s)

    lane = lax.broadcasted_iota(jnp.int32, (tt, nl), 1).astype(F32)
    neg = jnp.float32(-jnp.inf)
    vals = jnp.where(lane < N_EXPERTS, lg_ref[...], neg)
    tops, hots = [], []
    for _ in range(TOP_K):
        m = jnp.max(vals, axis=-1, keepdims=True)
        idx = jnp.min(jnp.where(vals == m, lane, float(nl)), axis=-1, keepdims=True)
        hot = lane == idx
        vals = jnp.where(hot, neg, vals)
        tops.append((m, idx))
        hots.append(hot)
    es = [jnp.exp(m - tops[0][0]) for m, _ in tops]
    den = es[0]
    for e in es[1:]:
        den = den + e

    mask = jnp.zeros((tt, nl), F32)
    for hot in hots:
        mask = jnp.where(hot, 1.0, mask)
    row = lax.broadcasted_iota(jnp.int32, (tt, tt), 0)
    col = lax.broadcasted_iota(jnp.int32, (tt, tt), 1)
    tri = jnp.where(col < row, 1.0, 0.0).astype(BF16)
    prefix = jnp.dot(tri, mask.astype(BF16), preferred_element_type=F32) + run_s[...]
    run_s[...] = run_s[...] + jnp.sum(mask, axis=0, keepdims=True)
    cnt_ref[...] = run_s[...]

    rec = jnp.zeros((tt, nl), F32)
    for k in range(TOP_K):
        rank = jnp.sum(jnp.where(hots[k], prefix, 0.0), axis=-1, keepdims=True)
        rec = jnp.where(lane == ROUTE_IDX + k, tops[k][1].astype(F32), rec)
        rec = jnp.where(lane == ROUTE_RANK + k, rank, rec)
        rec = jnp.where(lane == ROUTE_WT + k, es[k] / den, rec)
    rec_ref[...] = rec


def _route(logits):
    t, nl = logits.shape
    tt = min(ROUTE_TILE, t)
    while t % tt:
        tt -= SUBLANES
    return pl.pallas_call(
        _route_kernel, name="moe_route",
        grid=(t // tt,),
        in_specs=[pl.BlockSpec((tt, nl), lambda i: (i, 0))],
        out_specs=[pl.BlockSpec((tt, nl), lambda i: (i, 0)), pl.BlockSpec((1, nl), lambda i: (0, 0))],
        out_shape=(jax.ShapeDtypeStruct((t, nl), F32), jax.ShapeDtypeStruct((1, nl), F32)),
        scratch_shapes=[pltpu.VMEM((1, nl), F32)],
        compiler_params=_cp(("arbitrary",)),
    )(logits)


def _expert_kernel(te_ref, tv_ref, xs_ref, wgu_ref, bgu_ref, wd_ref, bd_ref, o_ref, wgu_s, wd_s):
    i = pl.program_id(0)
    de = wd_s.shape[0]
    changed = jnp.logical_or(i == 0, te_ref[i] != te_ref[jnp.maximum(i - 1, 0)])

    @pl.when(changed)
    def _():
        wgu_s[...] = wgu_ref[0, 0].astype(BF16)
        wd_s[...] = wd_ref[0, 0].astype(BF16)

    @pl.when(tv_ref[i] > 0)
    def _():
        gu = jnp.dot(xs_ref[...].astype(BF16), wgu_s[...], preferred_element_type=F32) + bgu_ref[0, 0]
        glu = jnp.minimum(gu[:, :de], SWIGLU_LIMIT)
        lin = jnp.clip(gu[:, de:], -SWIGLU_LIMIT, SWIGLU_LIMIT)
        act = glu * _sigmoid(SWIGLU_ALPHA * glu) * (lin + 1.0)
        o_ref[...] = jnp.dot(act.astype(BF16), wd_s[...], preferred_element_type=F32) + bd_ref[0, 0]

    @pl.when(tv_ref[i] == 0)
    def _():
        o_ref[...] = jnp.zeros_like(o_ref)


def _experts(tile_expert, tile_valid, xs, layer, w_gu, b_gu, w_down, b_down):
    rows, d = xs.shape
    depth, ne, _, n2 = w_gu.shape
    de = w_down.shape[2]
    tm = EXPERT_TILE
    grid_spec = pltpu.PrefetchScalarGridSpec(
        num_scalar_prefetch=2,
        grid=(rows // tm,),
        in_specs=[pl.BlockSpec((tm, d), lambda i, te, tv: (i, 0)),
                  pl.BlockSpec((1, 1, d, n2), lambda i, te, tv: (layer, te[i], 0, 0)),
                  pl.BlockSpec((1, 1, 1, n2), lambda i, te, tv: (layer, te[i], 0, 0)),
                  pl.BlockSpec((1, 1, de, d), lambda i, te, tv: (layer, te[i], 0, 0)),
                  pl.BlockSpec((1, 1, 1, d), lambda i, te, tv: (layer, te[i], 0, 0))],
        out_specs=pl.BlockSpec((tm, d), lambda i, te, tv: (i, 0)),
        scratch_shapes=[pltpu.VMEM((d, n2), BF16), pltpu.VMEM((de, d), BF16)])
    return pl.pallas_call(
        _expert_kernel, name="moe_experts",
        grid_spec=grid_spec,
        out_shape=jax.ShapeDtypeStruct((rows, d), F32),
        compiler_params=_cp(("arbitrary",)),
    )(tile_expert, tile_valid, xs, w_gu, b_gu.reshape(depth, ne, 1, n2), w_down, b_down.reshape(depth, ne, 1, d))


def _combine_kernel(x_ref, og0_ref, og1_ref, og2_ref, og3_ref, rec_ref, gt_ref, fg_ref, x2_ref, *, final):
    rec = rec_ref[...]
    y = rec[:, :, ROUTE_WT:ROUTE_WT + 1] * og0_ref[...]
    for k, og_ref in ((1, og1_ref), (2, og2_ref), (3, og3_ref)):
        y = y + rec[:, :, ROUTE_WT + k:ROUTE_WT + k + 1] * og_ref[...]
    x2 = x_ref[...] + gt_ref[...] * y
    if final:
        x2 = x2 * lax.rsqrt(jnp.mean(x2 * x2, axis=-1, keepdims=True) + RMS_EPS) * fg_ref[...]
    x2_ref[...] = x2


def _combine(x1, og_all, tok_off, rec, mod, final_g, final, tokens=256):
    seq, batch, d = x1.shape
    lt, bb = _tiles(seq, batch, tokens)
    nl = rec.shape[-1]
    t_all = og_all.shape[0] // TOP_K
    step = batch * lt
    if t_all % step == 0 and tok_off % step == 0:
        og = og_all.reshape(TOP_K * t_all // batch, batch, d)
        starts = [(k * t_all + tok_off) // step for k in range(TOP_K)]
    else:
        og = jnp.concatenate([og_all[k * t_all + tok_off:k * t_all + tok_off + seq * batch] for k in range(TOP_K)],
                             axis=0).reshape(TOP_K * seq, batch, d)
        starts = [k * (seq // lt) for k in range(TOP_K)]
    ogspec = lambda k: pl.BlockSpec((lt, bb, d), lambda b, t: (starts[k] + t, b, 0))
    return pl.pallas_call(
        functools.partial(_combine_kernel, final=final), name="moe_combine",
        grid=(batch // bb, seq // lt),
        in_specs=[pl.BlockSpec((lt, bb, d), lambda b, t: (t, b, 0)),
                  ogspec(0), ogspec(1), ogspec(2), ogspec(3),
                  pl.BlockSpec((lt, bb, nl), lambda b, t: (t, b, 0)),
                  pl.BlockSpec((1, bb, d), lambda b, t: (0, b, 5)),
                  pl.BlockSpec((1, 1, d), lambda b, t: (0, 0, 0))],
        out_specs=pl.BlockSpec((lt, bb, d), lambda b, t: (t, b, 0)),
        out_shape=jax.ShapeDtypeStruct((seq, batch, d), F32),
        compiler_params=_cp(("arbitrary", "arbitrary")),
    )(x1, og, og, og, og, rec, mod, final_g.reshape(1, 1, d))


def _pad_in_cols(a, axis):
    d = D_MODEL
    main = lax.slice_in_dim(a, 0, 6 * d, axis=axis)
    lora = lax.slice_in_dim(a, 6 * d, 6 * d + RW_LORA, axis=axis)
    gates = lax.slice_in_dim(a, 6 * d + RW_LORA, 9 * d + RW_LORA, axis=axis)
    pad_shape = list(a.shape)
    pad_shape[axis] = LORA_PAD - RW_LORA
    return jnp.concatenate([main, gates, lora, jnp.zeros(pad_shape, a.dtype)], axis=axis)


def _mix_group(x, mod, st, rw, layer, lp, s5p):
    seq, batch, d = x.shape
    s5_re, s5_im, lru_h, lru_conv, rw_shift = st
    z = _in_proj(x, mod, lp["norm1_g"], lp["w_in_p"])

    y_s5, n_s5_re, n_s5_im = _s5_mixer(
        z, s5p["ab_re"], s5p["ab_im"], s5p["bb_re"], s5p["bb_im"], s5p["ct_re"], s5p["ct_im"], s5p["d"],
        s5_re.reshape(batch, S5_LANES), s5_im.reshape(batch, S5_LANES))

    y_lru, n_lru_h, tail = _lru_mixer(
        z, lru_conv.transpose(1, 0, 2), lp["lru_conv_w"], lp["lru_conv_b"], lp["lru_w_a"], lp["lru_b_a"],
        lp["lru_w_x"], lp["lru_b_x"], lp["lru_lam"], lru_h)

    y_rw, rw_buf = _rwkv(z, rw_shift, rw[0], rw[1], rw[2], layer, lp)
    n_rw_shift = jnp.concatenate([z[seq - 1, :, COL_R:COL_R + 3 * d], z[seq - 1, :, COL_LORA:COL_LORA + RW_LORA]],
                                 axis=-1)

    x1, h2, logits = _merge(x, y_s5, y_lru, y_rw, z, mod, lp["s5_w_glu"], lp["s5_b_glu"], lp["w_br_s5"],
                            lp["w_br_lru"], lp["w_br_rw"], lp["w_out"], lp["norm2_g"], lp["wr_p"], lp["br_p"])
    new = (n_s5_re.reshape(batch, S5_GROUPS, S5_STATE), n_s5_im.reshape(batch, S5_GROUPS, S5_STATE),
           n_lru_h, tail.transpose(1, 0, 2), n_rw_shift)
    return x1, h2, logits, new, rw_buf


def _moe(h2_all, logits_all, layer, moe_params):
    t, d = h2_all.shape
    tm = EXPERT_TILE
    rec, cnt = _route(logits_all)
    idx = rec[:, ROUTE_IDX:ROUTE_IDX + TOP_K].astype(jnp.int32)
    rank = rec[:, ROUTE_RANK:ROUTE_RANK + TOP_K].astype(jnp.int32)
    counts = cnt[0, :N_EXPERTS].astype(jnp.int32)
    padded = ((counts + tm - 1) // tm) * tm
    ends = jnp.cumsum(padded)
    offs = ends - padded
    pos = offs[idx] + rank
    n_tiles = (t * TOP_K + N_EXPERTS * (tm - 1) + tm - 1) // tm
    rows = n_tiles * tm
    src = jnp.zeros((rows,), jnp.int32).at[pos.reshape(-1)].set(
        jnp.repeat(jnp.arange(t, dtype=jnp.int32), TOP_K), unique_indices=True, mode="promise_in_bounds")
    starts = jnp.arange(n_tiles, dtype=jnp.int32) * tm
    tile_valid = (starts < ends[-1]).astype(jnp.int32)
    owner = lambda row: jnp.sum((ends[None, :] <= row[:, None]).astype(jnp.int32), axis=1)
    last = owner(ends[-1:] - 1)[0]
    tile_expert = jnp.clip(jnp.where(tile_valid > 0, owner(starts), last), 0, N_EXPERTS - 1)
    xs = h2_all.at[src].get(mode="promise_in_bounds")
    os_ = _experts(tile_expert, tile_valid, xs, layer, *moe_params)
    og = os_.at[pos.T.reshape(-1)].get(mode="promise_in_bounds")
    return og, rec


def kernel(x_prompt, x_sample, state_s5_re, state_s5_im, state_lru_h, cache_lru_conv, state_rwkv, cache_rwkv_shift, c_prompt, c_sample, w_mod, b_mod, norm1_g, w_in, s5_lam_re, s5_lam_im, s5_log_step, s5_b_re, s5_b_im, s5_c_re, s5_c_im, s5_d, s5_w_glu, s5_b_glu, lru_conv_w, lru_conv_b, lru_w_a, lru_b_a, lru_w_x, lru_b_x, lru_lam, rw_mu, rw_w0, rw_w2, rw_a0, rw_a2, rw_g2, rw_k_k, rw_k_a, rw_r_k, rw_ln_w, rw_ln_b, w_br_s5, w_br_lru, w_br_rw, w_out, norm2_g, moe_w_router, moe_b_router, moe_w_gu, moe_b_gu, moe_w_down, moe_b_down, final_g):
    depth = w_mod.shape[0]
    d = D_MODEL
    bp, lp_len = x_prompt.shape[0], x_prompt.shape[1]
    bs, ls_len = x_sample.shape[0], x_sample.shape[1]

    mod_all = _modulation(jnp.concatenate([c_prompt, c_sample], axis=0), w_mod, b_mod)

    ab_re, ab_im, q_re, q_im = _s5_params(s5_lam_re, s5_lam_im, s5_log_step)
    shp = (depth, S5_GROUPS, S5_STATE)
    q_re, q_im = q_re.reshape(shp)[..., None], q_im.reshape(shp)[..., None]
    bbar_re = q_re * s5_b_re - q_im * s5_b_im
    bbar_im = q_re * s5_b_im + q_im * s5_b_re
    per = LANES // S5_GROUP

    xs = [_swap_leading(x_prompt), _swap_leading(x_sample)]
    zeros_like_state = lambda s, b: jnp.zeros((b,) + s.shape[2:], F32)
    sample_states = (state_s5_re, state_s5_im, state_lru_h, cache_lru_conv, cache_rwkv_shift)
    collected = [tuple([] for _ in sample_states), tuple([] for _ in sample_states)]
    rw_shape = state_rwkv.shape[2:]
    rw_src = [(jnp.zeros((1, bp) + rw_shape, F32), 0), (state_rwkv, None)]
    rw_bufs = [jnp.zeros((depth, bp) + rw_shape, F32), jnp.zeros((depth, bs) + rw_shape, F32)]

    for l in range(depth):
        lp = dict(
            norm1_g=norm1_g[l], w_in_p=_pad_in_cols(w_in[l], 1).astype(BF16),
            lru_conv_w=lru_conv_w[l], lru_conv_b=lru_conv_b[l], lru_w_a=lru_w_a[l], lru_b_a=lru_b_a[l],
            lru_w_x=lru_w_x[l], lru_b_x=lru_b_x[l], lru_lam=lru_lam[l],
            rw_mu=rw_mu[l], rw_w0=rw_w0[l], rw_a0=rw_a0[l],
            rw_w2p=jnp.concatenate([rw_w2[l], jnp.zeros((LANES - RW_W_LORA, d), F32)], axis=0).astype(BF16),
            rw_a2p=jnp.concatenate([jnp.zeros((RW_W_LORA, d), F32), rw_a2[l]], axis=0).astype(BF16),
            rw_g2p=jnp.concatenate([rw_g2[l], jnp.zeros((2 * LANES - RW_G_LORA, d), F32)], axis=0).astype(BF16),
            rw_k_k=rw_k_k[l], rw_k_a=rw_k_a[l], rw_r_k=rw_r_k[l], rw_ln_w=rw_ln_w[l], rw_ln_b=rw_ln_b[l],
            s5_w_glu=s5_w_glu[l].astype(BF16), s5_b_glu=s5_b_glu[l], w_br_s5=w_br_s5[l].astype(BF16),
            w_br_lru=w_br_lru[l].astype(BF16), w_br_rw=w_br_rw[l].astype(BF16), w_out=w_out[l].astype(BF16),
            norm2_g=norm2_g[l],
            wr_p=jnp.concatenate([moe_w_router[l], jnp.zeros((d, LANES - N_EXPERTS), F32)], axis=1),
            br_p=jnp.concatenate([moe_b_router[l], jnp.zeros((LANES - N_EXPERTS,), F32)]).reshape(1, LANES))
        g0 = l * S5_GROUPS
        s5p = dict(
            ab_re=ab_re[g0:g0 + S5_GROUPS].reshape(1, S5_LANES), ab_im=ab_im[g0:g0 + S5_GROUPS].reshape(1, S5_LANES),
            bb_re=_block_diag(bbar_re[l].transpose(0, 2, 1), per).astype(BF16),
            bb_im=_block_diag(bbar_im[l].transpose(0, 2, 1), per).astype(BF16),
            ct_re=_block_diag(s5_c_re[l].transpose(0, 2, 1), per).astype(BF16),
            ct_im=_block_diag(s5_c_im[l].transpose(0, 2, 1), per).astype(BF16),
            d=s5_d[l].reshape(1, d))

        mods = [mod_all[l, :bp][None], mod_all[l, bp:][None]]
        states = [tuple(zeros_like_state(s, bp) for s in sample_states), tuple(s[l] for s in sample_states)]
        x1s, h2s, lgs = [], [], []
        for gi in range(2):
            src, src_layer = rw_src[gi]
            rw = (src, l if src_layer is None else src_layer, rw_bufs[gi])
            x1, h2, lg, new, rw_bufs[gi] = _mix_group(xs[gi], mods[gi], states[gi], rw, l, lp, s5p)
            x1s.append(x1)
            h2s.append(h2.reshape(-1, d))
            lgs.append(lg.reshape(-1, LANES))
            for lst, s in zip(collected[gi], new):
                lst.append(s)

        og, rec = _moe(jnp.concatenate(h2s[::-1], axis=0), jnp.concatenate(lgs[::-1], axis=0), l,
                       (moe_w_gu, moe_b_gu, moe_w_down, moe_b_down))
        ts = ls_len * bs
        offs = [ts, 0]
        recs = [rec[ts:].reshape(lp_len, bp, LANES), rec[:ts].reshape(ls_len, bs, LANES)]
        xs = [_combine(x1s[gi], og, offs[gi], recs[gi], mods[gi], final_g, final=(l == depth - 1))
              for gi in range(2)]

    y_prompt = _swap_leading(xs[0])
    y_sample = _swap_leading(xs[1])
    outs = []
    for gi in range(2):
        st = [jnp.stack(lst) for lst in collected[gi]]
        outs += st[:4] + [rw_bufs[gi], st[4]]
    return (y_prompt, y_sample) + tuple(outs)
```

```python
import functools
import math

import jax
import jax.numpy as jnp
from jax import lax
from jax.experimental import pallas as pl
from jax.experimental.pallas import tpu as pltpu

F32 = jnp.float32
BF16 = jnp.bfloat16
HIGHEST = lax.Precision.HIGHEST

D_MODEL = 1024
RMS_EPS = 1e-5
S5_GROUP = 16
S5_GROUPS = D_MODEL // S5_GROUP
S5_STATE = 64
S5_LANES = S5_GROUPS * S5_STATE
LRU_HEADS = 8
LRU_BLOCK = D_MODEL // LRU_HEADS
CONV_WIDTH = 4
LRU_C = 8.0
RW_HEAD = 64
RW_HEADS = D_MODEL // RW_HEAD
RW_W_LORA = 64
RW_A_LORA = 64
RW_G_LORA = 160
RW_LORA = RW_W_LORA + RW_A_LORA + RW_G_LORA
RW_COLS = 3 * D_MODEL + RW_LORA
RW_LN_EPS = 64e-5
N_EXPERTS = 32
TOP_K = 4
SWIGLU_ALPHA = 1.702
SWIGLU_LIMIT = 7.0

LANES = 128
SUBLANES = 8
VMEM_LIMIT = 56 * 1024 * 1024

LORA_PAD = 512
COL_S5 = 0
COL_XLRU = 1 * D_MODEL
COL_GLRU = 2 * D_MODEL
COL_R = 3 * D_MODEL
COL_K = 4 * D_MODEL
COL_V = 5 * D_MODEL
COL_GATE = 6 * D_MODEL
COL_LORA = 9 * D_MODEL
D_IN_PAD = COL_LORA + LORA_PAD

S5_LANE_BLOCK = LANES
S5_STATE_BLOCK = (LANES // S5_GROUP) * S5_STATE
EXPERT_TILE = 512
ROUTE_TILE = 512


def _cp(sem, vmem=VMEM_LIMIT):
    return pltpu.CompilerParams(dimension_semantics=sem, vmem_limit_bytes=vmem)


def _gelu(x):
    return 0.5 * x * (1.0 + jnp.tanh(math.sqrt(2.0 / math.pi) * (x + 0.044715 * (x * x * x))))


def _sigmoid(x):
    return 1.0 / (1.0 + jnp.exp(-x))


def _softplus(x):
    return jnp.maximum(x, 0.0) + jnp.log1p(jnp.exp(-jnp.abs(x)))


def _tiles(seq, batch, tokens):
    bb = min(batch, max(SUBLANES, (tokens // seq) // SUBLANES * SUBLANES))
    while batch % bb:
        bb -= SUBLANES
    lt = max(1, min(seq, tokens // bb))
    while seq % lt:
        lt -= 1
    return lt, bb


def _swap_kernel(x_ref, o_ref):
    n0, n1, _ = x_ref.shape
    if n0 <= n1:
        for i in range(n0):
            o_ref[:, i, :] = x_ref[i]
    else:
        for i in range(n1):
            o_ref[i] = x_ref[:, i, :]


def _swap_leading(x, tokens=1024):
    a, b, d = x.shape
    ta = min(a, SUBLANES) if a <= b else min(a, max(SUBLANES, tokens // b))
    tb = min(b, max(SUBLANES, tokens // ta))
    while a % ta:
        ta -= SUBLANES
    while b % tb:
        tb -= SUBLANES
    return pl.pallas_call(
        _swap_kernel, name="swap_leading",
        grid=(a // ta, b // tb),
        in_specs=[pl.BlockSpec((ta, tb, d), lambda i, j: (i, j, 0))],
        out_specs=pl.BlockSpec((tb, ta, d), lambda i, j: (j, i, 0)),
        out_shape=jax.ShapeDtypeStruct((b, a, d), x.dtype),
        compiler_params=_cp(("arbitrary", "arbitrary")),
    )(x)


def _mod_kernel(c_ref, w_ref, b_ref, o_ref):
    c = c_ref[...]
    s = c * _sigmoid(c)
    o_ref[0] = jnp.dot(s, w_ref[0], precision=HIGHEST, preferred_element_type=F32) + b_ref[0]


def _modulation(c, w_mod, b_mod):
    depth, d, n = w_mod.shape
    bc = c.shape[0]
    tn = 1536
    return pl.pallas_call(
        _mod_kernel, name="adaln_mod",
        grid=(depth, n // tn),
        in_specs=[pl.BlockSpec((bc, d), lambda l, j: (0, 0)),
                  pl.BlockSpec((1, d, tn), lambda l, j: (l, 0, j)),
                  pl.BlockSpec((1, 1, tn), lambda l, j: (l, 0, j))],
        out_specs=pl.BlockSpec((1, bc, tn), lambda l, j: (l, 0, j)),
        out_shape=jax.ShapeDtypeStruct((depth, bc, n), F32),
        compiler_params=_cp(("arbitrary", "arbitrary")),
    )(c, w_mod, b_mod.reshape(depth, 1, n))


def _in_kernel(x_ref, sh_ref, sc_ref, g_ref, w_ref, o_ref, h_scr):
    lt, bb, d = x_ref.shape

    @pl.when(pl.program_id(2) == 0)
    def _():
        x = x_ref[...]
        y = x * lax.rsqrt(jnp.mean(x * x, axis=-1, keepdims=True) + RMS_EPS) * g_ref[...]
        h = y * (1.0 + sc_ref[...]) + sh_ref[...]
        h_scr[...] = h.reshape(lt * bb, d).astype(BF16)

    o = jnp.dot(h_scr[...], w_ref[...], preferred_element_type=F32)
    o_ref[...] = o.reshape(lt, bb, o.shape[-1])


def _in_proj(x, mod, norm_g, w_in_p, tokens=2048, tn=512):
    seq, batch, d = x.shape
    lt, bb = _tiles(seq, batch, tokens)
    n = w_in_p.shape[1]
    return pl.pallas_call(
        _in_kernel, name="in_proj",
        grid=(batch // bb, seq // lt, n // tn),
        in_specs=[pl.BlockSpec((lt, bb, d), lambda b, t, j: (t, b, 0)),
                  pl.BlockSpec((1, bb, d), lambda b, t, j: (0, b, 0)),
                  pl.BlockSpec((1, bb, d), lambda b, t, j: (0, b, 1)),
                  pl.BlockSpec((1, 1, d), lambda b, t, j: (0, 0, 0)),
                  pl.BlockSpec((d, tn), lambda b, t, j: (0, j))],
        out_specs=pl.BlockSpec((lt, bb, tn), lambda b, t, j: (t, b, j)),
        out_shape=jax.ShapeDtypeStruct((seq, batch, n), F32),
        scratch_shapes=[pltpu.VMEM((lt * bb, d), BF16)],
        compiler_params=_cp(("arbitrary", "arbitrary", "arbitrary")),
    )(x, mod, mod, norm_g.reshape(1, 1, d), w_in_p)


def _s5_kernel(u_ref, bbr_ref, bbi_ref, ar_ref, ai_ref, ctr_ref, cti_ref, d_ref, h0r_ref, h0i_ref,
               y_ref, hr_ref, hi_ref, sr, si):
    lt, bb, nl = u_ref.shape
    ns = sr.shape[-1]

    @pl.when(pl.program_id(2) == 0)
    def _():
        hr_ref[...] = h0r_ref[...]
        hi_ref[...] = h0i_ref[...]

    u2 = u_ref[...].reshape(lt * bb, nl)
    ub = u2.astype(BF16)
    sr[...] = jnp.dot(ub, bbr_ref[0], preferred_element_type=F32).reshape(lt, bb, ns)
    si[...] = jnp.dot(ub, bbi_ref[0], preferred_element_type=F32).reshape(lt, bb, ns)
    ar = jnp.broadcast_to(ar_ref[...], (bb, ns))
    ai = jnp.broadcast_to(ai_ref[...], (bb, ns))

    def body(i, carry):
        hr, hi = carry
        nr = ar * hr - ai * hi + sr[i]
        ni = ar * hi + ai * hr + si[i]
        sr[i] = nr
        si[i] = ni
        return nr, ni

    hr, hi = lax.fori_loop(0, lt, body, (hr_ref[...], hi_ref[...]), unroll=min(lt, 8))
    hr_ref[...] = hr
    hi_ref[...] = hi
    y = (jnp.dot(sr[...].reshape(lt * bb, ns).astype(BF16), ctr_ref[0], preferred_element_type=F32)
         - jnp.dot(si[...].reshape(lt * bb, ns).astype(BF16), cti_ref[0], preferred_element_type=F32)
         + d_ref[...] * u2)
    y_ref[...] = y.reshape(lt, bb, nl)


def _s5_params_kernel(lr_ref, li_ref, ls_ref, abr_ref, abi_ref, qr_ref, qi_ref):
    lr = lr_ref[...]
    li = li_ref[...]
    step = jnp.exp(ls_ref[...])
    mag = jnp.exp(lr * step)
    ab_re = mag * jnp.cos(li * step)
    ab_im = mag * jnp.sin(li * step)
    den = lr * lr + li * li
    abr_ref[...] = ab_re
    abi_ref[...] = ab_im
    qr_ref[...] = ((ab_re - 1.0) * lr + ab_im * li) / den
    qi_ref[...] = (ab_im * lr - (ab_re - 1.0) * li) / den


def _s5_params(lam_re, lam_im, log_step):
    depth, g, n = lam_re.shape
    shp = jax.ShapeDtypeStruct((depth * g, n), F32)
    return pl.pallas_call(_s5_params_kernel, name="s5_params", out_shape=(shp, shp, shp, shp))(
        lam_re.reshape(depth * g, n), lam_im.reshape(depth * g, n), log_step.reshape(depth * g, 1))


def _block_diag(blocks, per):
    g, a, b = blocks.shape
    x = blocks.reshape(g // per, per, a, b)
    eye = jnp.eye(per, dtype=blocks.dtype)
    return jnp.einsum("gpab,pq->gpaqb", x, eye).reshape(g // per, per * a, per * b)


def _s5_mixer(z, ab_re, ab_im, bb_re, bb_im, ct_re, ct_im, d_skip, h0_re, h0_im, tokens=1024, max_bb=32):
    seq, batch, _ = z.shape
    lt, bb = _tiles(seq, batch, tokens)
    if bb > max_bb and batch % max_bb == 0:
        lt, bb = _tiles(seq, batch, max_bb * seq)
    nl, ns = S5_LANE_BLOCK, S5_STATE_BLOCK
    nblk = D_MODEL // nl
    col0 = COL_S5 // nl
    y, hr, hi = pl.pallas_call(
        _s5_kernel, name="s5_mixer",
        grid=(batch // bb, nblk, seq // lt),
        in_specs=[pl.BlockSpec((lt, bb, nl), lambda b, j, t: (t, b, col0 + j)),
                  pl.BlockSpec((1, nl, ns), lambda b, j, t: (j, 0, 0)),
                  pl.BlockSpec((1, nl, ns), lambda b, j, t: (j, 0, 0)),
                  pl.BlockSpec((1, ns), lambda b, j, t: (0, j)),
                  pl.BlockSpec((1, ns), lambda b, j, t: (0, j)),
                  pl.BlockSpec((1, ns, nl), lambda b, j, t: (j, 0, 0)),
                  pl.BlockSpec((1, ns, nl), lambda b, j, t: (j, 0, 0)),
                  pl.BlockSpec((1, nl), lambda b, j, t: (0, j)),
                  pl.BlockSpec((bb, ns), lambda b, j, t: (b, j)),
                  pl.BlockSpec((bb, ns), lambda b, j, t: (b, j))],
        out_specs=[pl.BlockSpec((lt, bb, nl), lambda b, j, t: (t, b, j)),
                   pl.BlockSpec((bb, ns), lambda b, j, t: (b, j)),
                   pl.BlockSpec((bb, ns), lambda b, j, t: (b, j))],
        out_shape=(jax.ShapeDtypeStruct((seq, batch, D_MODEL), F32),
                   jax.ShapeDtypeStruct((batch, S5_LANES), F32),
                   jax.ShapeDtypeStruct((batch, S5_LANES), F32)),
        scratch_shapes=[pltpu.VMEM((lt, bb, ns), F32), pltpu.VMEM((lt, bb, ns), F32)],
        compiler_params=_cp(("arbitrary", "arbitrary", "arbitrary")),
    )(z, bb_re, bb_im, ab_re, ab_im, ct_re, ct_im, d_skip, h0_re, h0_im)
    return y, hr, hi


def _lru_kernel(x_ref, g_ref, cbuf_ref, cw_ref, cb_ref, wa_ref, ba_ref, wx_ref, bx_ref, lam_ref, h0_ref,
                y_ref, hn_ref, tail_ref, prev_s, a_s, h_s):
    lt, bb, w = x_ref.shape
    taps = cw_ref.shape[0]

    @pl.when(pl.program_id(2) == 0)
    def _():
        hn_ref[...] = h0_ref[...]
        prev_s[...] = cbuf_ref[...]

    xp = jnp.concatenate([prev_s[...], x_ref[...]], axis=0)
    xc = cb_ref[...] + xp[0:lt] * cw_ref[0]
    for tap in range(1, taps):
        xc = xc + xp[tap:tap + lt] * cw_ref[tap]
    prev_s[...] = xp[lt:lt + taps - 1]
    tail_ref[...] = xp[lt:lt + taps - 1]

    xc2 = xc.reshape(lt * bb, w)
    xb = xc2.astype(BF16)
    r = _sigmoid(jnp.dot(xb, wa_ref[0], preferred_element_type=F32) + ba_ref[...])
    i = _sigmoid(jnp.dot(xb, wx_ref[0], preferred_element_type=F32) + bx_ref[...])
    log_a = (-LRU_C) * r * _softplus(-lam_ref[...])
    a = jnp.exp(log_a)
    th = jnp.tanh(log_a)
    neg_expm1 = -2.0 * th / (1.0 - th)
    b = jnp.sqrt(neg_expm1) * (i * xc2)
    a_s[...] = a.reshape(lt, bb, w)
    h_s[...] = b.reshape(lt, bb, w)

    def body(t, h):
        h = a_s[t] * h + h_s[t]
        h_s[t] = h
        return h

    hn_ref[...] = lax.fori_loop(0, lt, body, hn_ref[...], unroll=min(lt, 8))
    y_ref[...] = h_s[...] * _gelu(g_ref[...])


def _lru_mixer(z, conv_buf_t, conv_w, conv_b, w_a, b_a, w_x, b_x, lam, h0, tokens=1024):
    seq, batch, _ = z.shape
    lt, bb = _tiles(seq, batch, tokens)
    w = LRU_BLOCK
    xcol, gcol = COL_XLRU // w, COL_GLRU // w
    taps = CONV_WIDTH
    vec = lambda a: a.reshape(1, D_MODEL)
    vspec = pl.BlockSpec((1, w), lambda b, h, t: (0, h))
    y, hn, tail = pl.pallas_call(
        _lru_kernel, name="rglru_mixer",
        grid=(batch // bb, LRU_HEADS, seq // lt),
        in_specs=[pl.BlockSpec((lt, bb, w), lambda b, h, t: (t, b, xcol + h)),
                  pl.BlockSpec((lt, bb, w), lambda b, h, t: (t, b, gcol + h)),
                  pl.BlockSpec((taps - 1, bb, w), lambda b, h, t: (0, b, h)),
                  pl.BlockSpec((taps, 1, w), lambda b, h, t: (0, 0, h)),
                  vspec,
                  pl.BlockSpec((1, w, w), lambda b, h, t: (h, 0, 0)),
                  vspec,
                  pl.BlockSpec((1, w, w), lambda b, h, t: (h, 0, 0)),
                  vspec, vspec,
                  pl.BlockSpec((bb, w), lambda b, h, t: (b, h))],
        out_specs=[pl.BlockSpec((lt, bb, w), lambda b, h, t: (t, b, h)),
                   pl.BlockSpec((bb, w), lambda b, h, t: (b, h)),
                   pl.BlockSpec((taps - 1, bb, w), lambda b, h, t: (0, b, h))],
        out_shape=(jax.ShapeDtypeStruct((seq, batch, D_MODEL), F32),
                   jax.ShapeDtypeStruct((batch, D_MODEL), F32),
                   jax.ShapeDtypeStruct((taps - 1, batch, D_MODEL), F32)),
        scratch_shapes=[pltpu.VMEM((taps - 1, bb, w), F32), pltpu.VMEM((lt, bb, w), F32),
                        pltpu.VMEM((lt, bb, w), F32)],
        compiler_params=_cp(("arbitrary", "arbitrary", "arbitrary")),
    )(z, z, conv_buf_t, conv_w.reshape(taps, 1, D_MODEL), vec(conv_b), w_a.astype(BF16), vec(b_a),
      w_x.astype(BF16), vec(b_x), vec(lam), h0)
    return y, hn, tail


def _rwkv_kernel(zr_ref, zk_ref, zv_ref, zl_ref, sh_ref, shl_ref, mu_ref, mul_ref, w0_ref, w2_ref, a0_ref, a2_ref,
                 g2_ref, kk_ref, ka_ref, rk_ref, lnw_ref, lnb_ref, s0_ref, sbuf_ref,
                 y_ref, sn_ref,
                 prev_s, prevl_s, s_scr, sa_s, nkk_s, r_s, w_s, k_s, v_s, b_s, y_s):
    lt, bb, wd = zr_ref.shape
    kp = wd // LANES
    hd = RW_HEAD
    nl = zl_ref.shape[-1]
    n = lt * bb
    ti = pl.program_id(2)

    @pl.when(ti == 0)
    def _():
        prev_s[...] = sh_ref[...]
        prevl_s[...] = shl_ref[...]
        for i in range(hd):
            rows = [s0_ref[0, :, 2 * c + par, i, :] for par in range(2) for c in range(kp)]
            s_scr[:, i, :] = jnp.concatenate(rows, axis=0).T

    def shifted(z_ref, prev, mu):
        z = z_ref[...]
        zp = jnp.concatenate([prev, z[0:lt - 1]], axis=0) if lt > 1 else prev
        return z + (zp - z) * mu, z[lt - 1:lt]

    r, last_r = shifted(zr_ref, prev_s[0:1], mu_ref[0:1])
    k, last_k = shifted(zk_ref, prev_s[1:2], mu_ref[1:2])
    v, last_v = shifted(zv_ref, prev_s[2:3], mu_ref[2:3])
    lo, last_l = shifted(zl_ref, prevl_s[...], mul_ref[...])
    prev_s[0:1] = last_r
    prev_s[1:2] = last_k
    prev_s[2:3] = last_v
    prevl_s[...] = last_l

    lo2 = lo.reshape(n, nl)
    wa_in = lo2[:, 0:LANES]
    g_in = lo2[:, LANES:3 * LANES]
    wpre = w0_ref[...] + jnp.dot(jnp.tanh(wa_in).astype(BF16), w2_ref[...], preferred_element_type=F32)
    decay = jnp.exp(-math.exp(-0.5) * _sigmoid(wpre))
    a = _sigmoid(a0_ref[...] + jnp.dot(wa_in.astype(BF16), a2_ref[...], preferred_element_type=F32))
    g = jnp.dot(_sigmoid(g_in).astype(BF16), g2_ref[...], preferred_element_type=F32)
    k2 = k.reshape(n, wd)
    shp = (lt, bb, wd)

    def to_lanes(q):
        rows = jnp.concatenate([q[:, :, c * LANES:(c + 1) * LANES] for c in range(kp)], axis=1) if kp > 1 else q
        t = jnp.swapaxes(rows, 1, 2)
        return jnp.concatenate([t[:, 0:hd, :], t[:, hd:2 * hd, :]], axis=2)

    r_s[...] = to_lanes(r)
    w_s[...] = to_lanes(decay.reshape(shp))
    k_s[...] = to_lanes((k2 * (1.0 + (a - 1.0) * ka_ref[...])).reshape(shp))
    v_s[...] = to_lanes(v)
    kk = to_lanes((k2 * kk_ref[...]).reshape(shp))
    kkn = kk * lax.rsqrt(jnp.maximum(jnp.sum(kk * kk, axis=1, keepdims=True), 1e-24))
    nkk_s[0:lt] = -kkn
    nkk_s[lt:lt + 1] = jnp.zeros((1, hd, LANES), F32)
    b_s[...] = kkn * to_lanes(a.reshape(shp))

    acc0 = [jnp.zeros((hd, LANES), F32) for _ in range(2)]
    for j in range(hd):
        acc0[j % 2] = acc0[j % 2] + s_scr[j] * nkk_s[0, pl.ds(j, 1), :]
    sa_s[...] = acc0[0] + acc0[1]
    rg = hd // 2

    def step(t, carry):
        row = lambda ref, tt, j: ref[tt, pl.ds(j, 1), :]

        def group(g, c2):
            rows = pl.ds(pl.multiple_of(g * rg, rg), rg)
            sa = sa_s[rows, :]
            vt = v_s[t, rows, :]
            nac = [jnp.zeros((rg, LANES), F32) for _ in range(2)]
            yac = [jnp.zeros((rg, LANES), F32) for _ in range(2)]
            for j in range(hd):
                sj = s_scr[j, rows, :] * row(w_s, t, j) + sa * row(b_s, t, j) + vt * row(k_s, t, j)
                s_scr[j, rows, :] = sj
                yac[j % 2] = yac[j % 2] + sj * row(r_s, t, j)
                nac[j % 2] = nac[j % 2] + sj * row(nkk_s, t + 1, j)
            y_s[t, rows, :] = yac[0] + yac[1]
            sa_s[rows, :] = nac[0] + nac[1]
            return c2

        lax.fori_loop(0, hd // rg, group, 0)
        return carry

    lax.fori_loop(0, lt, step, 0)

    ys = y_s[...]
    mean = jnp.mean(ys, axis=1, keepdims=True)
    yc = ys - mean
    var = jnp.mean(yc * yc, axis=1, keepdims=True)
    yn = yc * lax.rsqrt(var + RW_LN_EPS)
    bonus = jnp.sum(r_s[...] * k_s[...] * rk_ref[...], axis=1, keepdims=True)
    o = yn * lnw_ref[...] + lnb_ref[...] + bonus * v_s[...]
    o = jnp.swapaxes(jnp.concatenate([o[:, :, 0:hd], o[:, :, hd:2 * hd]], axis=1), 1, 2)
    nat = jnp.concatenate([o[:, c * bb:(c + 1) * bb, :] for c in range(kp)], axis=2) if kp > 1 else o
    y_ref[...] = nat * g.reshape(shp)

    @pl.when(ti == pl.num_programs(2) - 1)
    def _():
        for i in range(hd):
            m = s_scr[:, i, :].T
            for par in range(2):
                for c in range(kp):
                    r0 = (par * kp + c) * bb
                    sn_ref[0, :, 2 * c + par, i, :] = m[r0:r0 + bb, :]


def _rwkv(z, rw_shift, s_in, s_layer, s_out, layer, lp, lt=32):
    seq, batch, _ = z.shape
    d, hd, nl = D_MODEL, RW_HEAD, LORA_PAD
    bb, kp = _rw_tiles(batch)
    wd = LANES * kp
    n_pg = d // wd
    lt = min(lt, seq)
    while seq % lt:
        lt -= 1
    pad = lambda a: jnp.concatenate([a, jnp.zeros(a.shape[:-1] + (nl - RW_LORA,), a.dtype)], axis=-1)
    sh = rw_shift[:, 0:3 * d].reshape(batch, 3, d).transpose(1, 0, 2)
    shl = pad(rw_shift[:, 3 * d:])[None]
    mu = lp["rw_mu"][0:3 * d].reshape(3, 1, d)
    mul = pad(lp["rw_mu"][3 * d:]).reshape(1, 1, nl)
    vec = lambda a: a.reshape(1, d)
    zspec = lambda col: pl.BlockSpec((lt, bb, wd), lambda b, g, t: (t, b, col // wd + g))
    vspec = pl.BlockSpec((1, wd), lambda b, g, t: (0, g))
    wspec = lambda rows: pl.BlockSpec((rows, wd), lambda b, g, t: (0, g))
    pspec = pl.BlockSpec((1, hd, LANES), lambda b, g, t: (g, 0, 0))
    sspec = lambda lyr: pl.BlockSpec((1, bb, 2 * kp, hd, hd), lambda b, g, t: (lyr, b, g, 0, 0))
    chunk = pltpu.VMEM((lt, hd, LANES), F32)
    y, sn = pl.pallas_call(
        _rwkv_kernel, name="rwkv7_mixer",
        grid=(batch // bb, n_pg, seq // lt),
        in_specs=[zspec(COL_R), zspec(COL_K), zspec(COL_V),
                  pl.BlockSpec((lt, bb, nl), lambda b, g, t: (t, b, COL_LORA // nl)),
                  pl.BlockSpec((3, bb, wd), lambda b, g, t: (0, b, g)),
                  pl.BlockSpec((1, bb, nl), lambda b, g, t: (0, b, 0)),
                  pl.BlockSpec((3, 1, wd), lambda b, g, t: (0, 0, g)),
                  pl.BlockSpec((1, 1, nl), lambda b, g, t: (0, 0, 0)),
                  vspec, wspec(LANES), vspec, wspec(LANES), wspec(2 * LANES), vspec, vspec,
                  pspec, pspec, pspec, sspec(s_layer), pl.BlockSpec(memory_space=pl.ANY)],
        out_specs=[pl.BlockSpec((lt, bb, wd), lambda b, g, t: (t, b, g)), sspec(layer)],
        out_shape=(jax.ShapeDtypeStruct((seq, batch, d), F32),
                   jax.ShapeDtypeStruct(s_out.shape, F32)),
        input_output_aliases={19: 1},
        scratch_shapes=[pltpu.VMEM((3, bb, wd), F32), pltpu.VMEM((1, bb, nl), F32),
                        pltpu.VMEM((hd, hd, LANES), F32), pltpu.VMEM((hd, LANES), F32),
                        pltpu.VMEM((lt + 1, hd, LANES), F32)] + [chunk] * 6,
        compiler_params=_cp(("arbitrary", "arbitrary", "arbitrary")),
    )(z, z, z, z, sh, shl, mu, mul, vec(lp["rw_w0"]), lp["rw_w2p"], vec(lp["rw_a0"]), lp["rw_a2p"], lp["rw_g2p"],
      vec(lp["rw_k_k"]), vec(lp["rw_k_a"]), _rw_param_blocks(lp["rw_r_k"].reshape(-1), bb, kp),
      _rw_param_blocks(lp["rw_ln_w"], bb, kp), _rw_param_blocks(lp["rw_ln_b"], bb, kp), s_in, s_out)
    return y, sn


def _rw_tiles(batch):
    bb = min(batch, LANES // 2)
    assert (LANES // 2) % bb == 0 and batch % bb == 0 and bb % SUBLANES == 0, batch
    return bb, (LANES // 2) // bb


def _rw_param_blocks(p, bb, kp):
    n_pg = RW_HEADS // (2 * kp)
    x = p.reshape(n_pg, kp, 2, RW_HEAD).transpose(0, 3, 2, 1)
    return jnp.broadcast_to(x[..., None], (n_pg, RW_HEAD, 2, kp, bb)).reshape(n_pg, RW_HEAD, LANES)


def _merge_kernel(x_ref, ys5_ref, ylru_ref, yrw_ref, zg1_ref, zg2_ref, zg3_ref,
                  gt1_ref, sc2_ref, sh2_ref, wglu_ref, bglu_ref, wb1_ref, wb2_ref, wb3_ref, wout_ref,
                  n2g_ref, wr_ref, br_ref,
                  x1_ref, h2_ref, lg_ref):
    lt, bb, d = x_ref.shape
    n = lt * bb
    mm = lambda a, w_ref: jnp.dot(a.astype(BF16), w_ref[...], preferred_element_type=F32)
    flat = lambda ref: ref[...].reshape(n, d)

    y1 = _gelu(flat(ys5_ref))
    y1 = y1 * _sigmoid(mm(y1, wglu_ref) + bglu_ref[...])
    y3 = flat(yrw_ref)
    merged = (_sigmoid(flat(zg1_ref)) * mm(y1, wb1_ref)
              + _sigmoid(flat(zg2_ref)) * mm(flat(ylru_ref), wb2_ref)
              + _sigmoid(flat(zg3_ref)) * mm(y3, wb3_ref))
    upd = mm(merged, wout_ref).reshape(lt, bb, d)
    x1 = x_ref[...] + gt1_ref[...] * upd
    x1_ref[...] = x1
    y = x1 * lax.rsqrt(jnp.mean(x1 * x1, axis=-1, keepdims=True) + RMS_EPS) * n2g_ref[...]
    h2 = y * (1.0 + sc2_ref[...]) + sh2_ref[...]
    h2_ref[...] = h2
    logits = jnp.dot(h2.reshape(n, d), wr_ref[...], precision=HIGHEST, preferred_element_type=F32)
    lg_ref[...] = (logits + br_ref[...]).reshape(lt, bb, lg_ref.shape[-1])


def _merge(x, y_s5, y_lru, y_rw, z, mod, w_glu, b_glu, wb1, wb2, wb3, w_out, norm2_g, wr_p, br_p,
           tokens=512):
    seq, batch, d = x.shape
    lt, bb = _tiles(seq, batch, tokens)
    act = pl.BlockSpec((lt, bb, d), lambda b, t: (t, b, 0))
    gate =lambda i: pl.BlockSpec((lt, bb, d), lambda b, t: (t, b, COL_GATE // d + i))
    modspec = lambda i: pl.BlockSpec((1, bb, d), lambda b, t: (0, b, i))
    wspec = pl.BlockSpec((d, d), lambda b, t: (0, 0), pipeline_mode=pl.Buffered(1))
    vspec = pl.BlockSpec((1, d), lambda b, t: (0, 0))
    ne = wr_p.shape[1]
    return pl.pallas_call(
        _merge_kernel, name="merge_norm2_router",
        grid=(batch // bb, seq // lt),
        in_specs=[act] * 4 + [gate(0), gate(1), gate(2), modspec(2), modspec(4), modspec(3),
                              wspec, vspec, wspec, wspec, wspec, wspec,
                              pl.BlockSpec((1, 1, d), lambda b, t: (0, 0, 0)),
                              pl.BlockSpec((d, ne), lambda b, t: (0, 0)),
                              pl.BlockSpec((1, ne), lambda b, t: (0, 0))],
        out_specs=[act, act, pl.BlockSpec((lt, bb, ne), lambda b, t: (t, b, 0))],
        out_shape=(jax.ShapeDtypeStruct((seq, batch, d), F32),
                   jax.ShapeDtypeStruct((seq, batch, d), F32),
                   jax.ShapeDtypeStruct((seq, batch, ne), F32)),
        compiler_params=_cp(("arbitrary", "arbitrary")),
    )(x, y_s5, y_lru, y_rw, z, z, z, mod, mod, mod, w_glu, b_glu.reshape(1, d), wb1, wb2, wb3, w_out,
      norm2_g.reshape(1, 1, d), wr_p, br_p)


ROUTE_IDX, ROUTE_RANK, ROUTE_WT = 0, TOP_K, 2 * TOP_K


def _route_kernel(lg_ref, rec_ref, cnt_ref, run_s):
    tt, nl = lg_ref.shape

    @pl.when(pl.program_id(0) == 0)
    def _():
        run_s[...] = jnp.zeros_like(run_s)

    lane = lax.broadcasted_iota(jnp.int32, (tt, nl), 1).astype(F32)
    neg = jnp.float32(-jnp.inf)
    vals = jnp.where(lane < N_EXPERTS, lg_ref[...], neg)
    tops, hots = [], []
    for _ in range(TOP_K):
        m = jnp.max(vals, axis=-1, keepdims=True)
        idx = jnp.min(jnp.where(vals == m, lane, float(nl)), axis=-1, keepdims=True)
        hot = lane == idx
        vals = jnp.where(hot, neg, vals)
        tops.append((m, idx))
        hots.append(hot)
    es = [jnp.exp(m - tops[0][0]) for m, _ in tops]
    den = es[0]
    for e in es[1:]:
        den = den + e

    mask = jnp.zeros((tt, nl), F32)
    for hot in hots:
        mask = jnp.where(hot, 1.0, mask)
    row = lax.broadcasted_iota(jnp.int32, (tt, tt), 0)
    col = lax.broadcasted_iota(jnp.int32, (tt, tt), 1)
    tri = jnp.where(col < row, 1.0, 0.0).astype(BF16)
    prefix = jnp.dot(tri, mask.astype(BF16), preferred_element_type=F32) + run_s[...]
    run_s[...] = run_s[...] + jnp.sum(mask, axis=0, keepdims=True)
    cnt_ref[...] = run_s[...]

    rec = jnp.zeros((tt, nl), F32)
    for k in range(TOP_K):
        rank = jnp.sum(jnp.where(hots[k], prefix, 0.0), axis=-1, keepdims=True)
        rec = jnp.where(lane == ROUTE_IDX + k, tops[k][1].astype(F32), rec)
        rec = jnp.where(lane == ROUTE_RANK + k, rank, rec)
        rec = jnp.where(lane == ROUTE_WT + k, es[k] / den, rec)
    rec_ref[...] = rec


def _route(logits):
    t, nl = logits.shape
    tt = min(ROUTE_TILE, t)
    while t % tt:
        tt -= SUBLANES
    return pl.pallas_call(
        _route_kernel, name="moe_route",
        grid=(t // tt,),
        in_specs=[pl.BlockSpec((tt, nl), lambda i: (i, 0))],
        out_specs=[pl.BlockSpec((tt, nl), lambda i: (i, 0)), pl.BlockSpec((1, nl), lambda i: (0, 0))],
        out_shape=(jax.ShapeDtypeStruct((t, nl), F32), jax.ShapeDtypeStruct((1, nl), F32)),
        scratch_shapes=[pltpu.VMEM((1, nl), F32)],
        compiler_params=_cp(("arbitrary",)),
    )(logits)


def _expert_kernel(te_ref, tv_ref, xs_ref, wgu_ref, bgu_ref, wd_ref, bd_ref, o_ref, wgu_s, wd_s):
    i = pl.program_id(0)
    de = wd_s.shape[0]
    changed = jnp.logical_or(i == 0, te_ref[i] != te_ref[jnp.maximum(i - 1, 0)])

    @pl.when(changed)
    def _():
        wgu_s[...] = wgu_ref[0, 0].astype(BF16)
        wd_s[...] = wd_ref[0, 0].astype(BF16)

    @pl.when(tv_ref[i] > 0)
    def _():
        gu = jnp.dot(xs_ref[...].astype(BF16), wgu_s[...], preferred_element_type=F32) + bgu_ref[0, 0]
        glu = jnp.minimum(gu[:, :de], SWIGLU_LIMIT)
        lin = jnp.clip(gu[:, de:], -SWIGLU_LIMIT, SWIGLU_LIMIT)
        act = glu * _sigmoid(SWIGLU_ALPHA * glu) * (lin + 1.0)
        o_ref[...] = jnp.dot(act.astype(BF16), wd_s[...], preferred_element_type=F32) + bd_ref[0, 0]

    @pl.when(tv_ref[i] == 0)
    def _():
        o_ref[...] = jnp.zeros_like(o_ref)


def _experts(tile_expert, tile_valid, xs, layer, w_gu, b_gu, w_down, b_down):
    rows, d = xs.shape
    depth, ne, _, n2 = w_gu.shape
    de = w_down.shape[2]
    tm = EXPERT_TILE
    grid_spec = pltpu.PrefetchScalarGridSpec(
        num_scalar_prefetch=2,
        grid=(rows // tm,),
        in_specs=[pl.BlockSpec((tm, d), lambda i, te, tv: (i, 0)),
                  pl.BlockSpec((1, 1, d, n2), lambda i, te, tv: (layer, te[i], 0, 0)),
                  pl.BlockSpec((1, 1, 1, n2), lambda i, te, tv: (layer, te[i], 0, 0)),
                  pl.BlockSpec((1, 1, de, d), lambda i, te, tv: (layer, te[i], 0, 0)),
                  pl.BlockSpec((1, 1, 1, d), lambda i, te, tv: (layer, te[i], 0, 0))],
        out_specs=pl.BlockSpec((tm, d), lambda i, te, tv: (i, 0)),
        scratch_shapes=[pltpu.VMEM((d, n2), BF16), pltpu.VMEM((de, d), BF16)])
    return pl.pallas_call(
        _expert_kernel, name="moe_experts",
        grid_spec=grid_spec,
        out_shape=jax.ShapeDtypeStruct((rows, d), F32),
        compiler_params=_cp(("arbitrary",)),
    )(tile_expert, tile_valid, xs, w_gu, b_gu.reshape(depth, ne, 1, n2), w_down, b_down.reshape(depth, ne, 1, d))


def _combine_kernel(x_ref, og0_ref, og1_ref, og2_ref, og3_ref, rec_ref, gt_ref, fg_ref, x2_ref, *, final):
    rec = rec_ref[...]
    y = rec[:, :, ROUTE_WT:ROUTE_WT + 1] * og0_ref[...]
    for k, og_ref in ((1, og1_ref), (2, og2_ref), (3, og3_ref)):
        y = y + rec[:, :, ROUTE_WT + k:ROUTE_WT + k + 1] * og_ref[...]
    x2 = x_ref[...] + gt_ref[...] * y
    if final:
        x2 = x2 * lax.rsqrt(jnp.mean(x2 * x2, axis=-1, keepdims=True) + RMS_EPS) * fg_ref[...]
    x2_ref[...] = x2


def _combine(x1, og_all, tok_off, rec, mod, final_g, final, tokens=256):
    seq, batch, d = x1.shape
    lt, bb = _tiles(seq, batch, tokens)
    nl = rec.shape[-1]
    t_all = og_all.shape[0] // TOP_K
    step = batch * lt
    if t_all % step == 0 and tok_off % step == 0:
        og = og_all.reshape(TOP_K * t_all // batch, batch, d)
        starts = [(k * t_all + tok_off) // step for k in range(TOP_K)]
    else:
        og = jnp.concatenate([og_all[k * t_all + tok_off:k * t_all + tok_off + seq * batch] for k in range(TOP_K)],
                             axis=0).reshape(TOP_K * seq, batch, d)
        starts = [k * (seq // lt) for k in range(TOP_K)]
    ogspec = lambda k: pl.BlockSpec((lt, bb, d), lambda b, t: (starts[k] + t, b, 0))
    return pl.pallas_call(
        functools.partial(_combine_kernel, final=final), name="moe_combine",
        grid=(batch // bb, seq // lt),
        in_specs=[pl.BlockSpec((lt, bb, d), lambda b, t: (t, b, 0)),
                  ogspec(0), ogspec(1), ogspec(2), ogspec(3),
                  pl.BlockSpec((lt, bb, nl), lambda b, t: (t, b, 0)),
                  pl.BlockSpec((1, bb, d), lambda b, t: (0, b, 5)),
                  pl.BlockSpec((1, 1, d), lambda b, t: (0, 0, 0))],
        out_specs=pl.BlockSpec((lt, bb, d), lambda b, t: (t, b, 0)),
        out_shape=jax.ShapeDtypeStruct((seq, batch, d), F32),
        compiler_params=_cp(("arbitrary", "arbitrary")),
    )(x1, og, og, og, og, rec, mod, final_g.reshape(1, 1, d))


def _pad_in_cols(a, axis):
    d = D_MODEL
    main = lax.slice_in_dim(a, 0, 6 * d, axis=axis)
    lora = lax.slice_in_dim(a, 6 * d, 6 * d + RW_LORA, axis=axis)
    gates = lax.slice_in_dim(a, 6 * d + RW_LORA, 9 * d + RW_LORA, axis=axis)
    pad_shape = list(a.shape)
    pad_shape[axis] = LORA_PAD - RW_LORA
    return jnp.concatenate([main, gates, lora, jnp.zeros(pad_shape, a.dtype)], axis=axis)


def _mix_group(x, mod, st, rw, layer, lp, s5p):
    seq, batch, d = x.shape
    s5_re, s5_im, lru_h, lru_conv, rw_shift = st
    z = _in_proj(x, mod, lp["norm1_g"], lp["w_in_p"])

    y_s5, n_s5_re, n_s5_im = _s5_mixer(
        z, s5p["ab_re"], s5p["ab_im"], s5p["bb_re"], s5p["bb_im"], s5p["ct_re"], s5p["ct_im"], s5p["d"],
        s5_re.reshape(batch, S5_LANES), s5_im.reshape(batch, S5_LANES))

    y_lru, n_lru_h, tail = _lru_mixer(
        z, lru_conv.transpose(1, 0, 2), lp["lru_conv_w"], lp["lru_conv_b"], lp["lru_w_a"], lp["lru_b_a"],
        lp["lru_w_x"], lp["lru_b_x"], lp["lru_lam"], lru_h)

    y_rw, rw_buf = _rwkv(z, rw_shift, rw[0], rw[1], rw[2], layer, lp)
    n_rw_shift = jnp.concatenate([z[seq - 1, :, COL_R:COL_R + 3 * d], z[seq - 1, :, COL_LORA:COL_LORA + RW_LORA]],
                                 axis=-1)

    x1, h2, logits = _merge(x, y_s5, y_lru, y_rw, z, mod, lp["s5_w_glu"], lp["s5_b_glu"], lp["w_br_s5"],
                            lp["w_br_lru"], lp["w_br_rw"], lp["w_out"], lp["norm2_g"], lp["wr_p"], lp["br_p"])
    new = (n_s5_re.reshape(batch, S5_GROUPS, S5_STATE), n_s5_im.reshape(batch, S5_GROUPS, S5_STATE),
           n_lru_h, tail.transpose(1, 0, 2), n_rw_shift)
    return x1, h2, logits, new, rw_buf


def _moe(h2_all, logits_all, layer, moe_params):
    t, d = h2_all.shape
    tm = EXPERT_TILE
    rec, cnt = _route(logits_all)
    idx = rec[:, ROUTE_IDX:ROUTE_IDX + TOP_K].astype(jnp.int32)
    rank = rec[:, ROUTE_RANK:ROUTE_RANK + TOP_K].astype(jnp.int32)
    counts = cnt[0, :N_EXPERTS].astype(jnp.int32)
    padded = ((counts + tm - 1) // tm) * tm
    ends = jnp.cumsum(padded)
    offs = ends - padded
    pos = offs[idx] + rank
    n_tiles = (t * TOP_K + N_EXPERTS * (tm - 1) + tm - 1) // tm
    rows = n_tiles * tm
    src = jnp.zeros((rows,), jnp.int32).at[pos.reshape(-1)].set(
        jnp.repeat(jnp.arange(t, dtype=jnp.int32), TOP_K), unique_indices=True, mode="promise_in_bounds")
    starts = jnp.arange(n_tiles, dtype=jnp.int32) * tm
    tile_valid = (starts < ends[-1]).astype(jnp.int32)
    owner = lambda row: jnp.sum((ends[None, :] <= row[:, None]).astype(jnp.int32), axis=1)
    last = owner(ends[-1:] - 1)[0]
    tile_expert = jnp.clip(jnp.where(tile_valid > 0, owner(starts), last), 0, N_EXPERTS - 1)
    xs = h2_all.at[src].get(mode="promise_in_bounds")
    os_ = _experts(tile_expert, tile_valid, xs, layer, *moe_params)
    og = os_.at[pos.T.reshape(-1)].get(mode="promise_in_bounds")
    return og, rec


def kernel(x_prompt, x_sample, state_s5_re, state_s5_im, state_lru_h, cache_lru_conv, state_rwkv, cache_rwkv_shift, c_prompt, c_sample, w_mod, b_mod, norm1_g, w_in, s5_lam_re, s5_lam_im, s5_log_step, s5_b_re, s5_b_im, s5_c_re, s5_c_im, s5_d, s5_w_glu, s5_b_glu, lru_conv_w, lru_conv_b, lru_w_a, lru_b_a, lru_w_x, lru_b_x, lru_lam, rw_mu, rw_w0, rw_w2, rw_a0, rw_a2, rw_g2, rw_k_k, rw_k_a, rw_r_k, rw_ln_w, rw_ln_b, w_br_s5, w_br_lru, w_br_rw, w_out, norm2_g, moe_w_router, moe_b_router, moe_w_gu, moe_b_gu, moe_w_down, moe_b_down, final_g):
    depth = w_mod.shape[0]
    d = D_MODEL
    bp, lp_len = x_prompt.shape[0], x_prompt.shape[1]
    bs, ls_len = x_sample.shape[0], x_sample.shape[1]

    mod_all = _modulation(jnp.concatenate([c_prompt, c_sample], axis=0), w_mod, b_mod)

    ab_re, ab_im, q_re, q_im = _s5_params(s5_lam_re, s5_lam_im, s5_log_step)
    shp = (depth, S5_GROUPS, S5_STATE)
    q_re, q_im = q_re.reshape(shp)[..., None], q_im.reshape(shp)[..., None]
    bbar_re = q_re * s5_b_re - q_im * s5_b_im
    bbar_im = q_re * s5_b_im + q_im * s5_b_re
    per = LANES // S5_GROUP

    xs = [_swap_leading(x_prompt), _swap_leading(x_sample)]
    zeros_like_state = lambda s, b: jnp.zeros((b,) + s.shape[2:], F32)
    sample_states = (state_s5_re, state_s5_im, state_lru_h, cache_lru_conv, cache_rwkv_shift)
    collected = [tuple([] for _ in sample_states), tuple([] for _ in sample_states)]
    rw_shape = state_rwkv.shape[2:]
    rw_src = [(jnp.zeros((1, bp) + rw_shape, F32), 0), (state_rwkv, None)]
    rw_bufs = [jnp.zeros((depth, bp) + rw_shape, F32), jnp.zeros((depth, bs) + rw_shape, F32)]

    for l in range(depth):
        lp = dict(
            norm1_g=norm1_g[l], w_in_p=_pad_in_cols(w_in[l], 1).astype(BF16),
            lru_conv_w=lru_conv_w[l], lru_conv_b=lru_conv_b[l], lru_w_a=lru_w_a[l], lru_b_a=lru_b_a[l],
            lru_w_x=lru_w_x[l], lru_b_x=lru_b_x[l], lru_lam=lru_lam[l],
            rw_mu=rw_mu[l], rw_w0=rw_w0[l], rw_a0=rw_a0[l],
            rw_w2p=jnp.concatenate([rw_w2[l], jnp.zeros((LANES - RW_W_LORA, d), F32)], axis=0).astype(BF16),
            rw_a2p=jnp.concatenate([jnp.zeros((RW_W_LORA, d), F32), rw_a2[l]], axis=0).astype(BF16),
            rw_g2p=jnp.concatenate([rw_g2[l], jnp.zeros((2 * LANES - RW_G_LORA, d), F32)], axis=0).astype(BF16),
            rw_k_k=rw_k_k[l], rw_k_a=rw_k_a[l], rw_r_k=rw_r_k[l], rw_ln_w=rw_ln_w[l], rw_ln_b=rw_ln_b[l],
            s5_w_glu=s5_w_glu[l].astype(BF16), s5_b_glu=s5_b_glu[l], w_br_s5=w_br_s5[l].astype(BF16),
            w_br_lru=w_br_lru[l].astype(BF16), w_br_rw=w_br_rw[l].astype(BF16), w_out=w_out[l].astype(BF16),
            norm2_g=norm2_g[l],
            wr_p=jnp.concatenate([moe_w_router[l], jnp.zeros((d, LANES - N_EXPERTS), F32)], axis=1),
            br_p=jnp.concatenate([moe_b_router[l], jnp.zeros((LANES - N_EXPERTS,), F32)]).reshape(1, LANES))
        g0 = l * S5_GROUPS
        s5p = dict(
            ab_re=ab_re[g0:g0 + S5_GROUPS].reshape(1, S5_LANES), ab_im=ab_im[g0:g0 + S5_GROUPS].reshape(1, S5_LANES),
            bb_re=_block_diag(bbar_re[l].transpose(0, 2, 1), per).astype(BF16),
            bb_im=_block_diag(bbar_im[l].transpose(0, 2, 1), per).astype(BF16),
            ct_re=_block_diag(s5_c_re[l].transpose(0, 2, 1), per).astype(BF16),
            ct_im=_block_diag(s5_c_im[l].transpose(0, 2, 1), per).astype(BF16),
            d=s5_d[l].reshape(1, d))

        mods = [mod_all[l, :bp][None], mod_all[l, bp:][None]]
        states = [tuple(zeros_like_state(s, bp) for s in sample_states), tuple(s[l] for s in sample_states)]
        x1s, h2s, lgs = [], [], []
        for gi in range(2):
            src, src_layer = rw_src[gi]
            rw = (src, l if src_layer is None else src_layer, rw_bufs[gi])
            x1, h2, lg, new, rw_bufs[gi] = _mix_group(xs[gi], mods[gi], states[gi], rw, l, lp, s5p)
            x1s.append(x1)
            h2s.append(h2.reshape(-1, d))
            lgs.append(lg.reshape(-1, LANES))
            for lst, s in zip(collected[gi], new):
                lst.append(s)

        og, rec = _moe(jnp.concatenate(h2s[::-1], axis=0), jnp.concatenate(lgs[::-1], axis=0), l,
                       (moe_w_gu, moe_b_gu, moe_w_down, moe_b_down))
        ts = ls_len * bs
        offs = [ts, 0]
        recs = [rec[ts:].reshape(lp_len, bp, LANES), rec[:ts].reshape(ls_len, bs, LANES)]
        xs = [_combine(x1s[gi], og, offs[gi], recs[gi], mods[gi], final_g, final=(l == depth - 1))
              for gi in range(2)]

    y_prompt = _swap_leading(xs[0])
    y_sample = _swap_leading(xs[1])
    outs = []
    for gi in range(2):
        st = [jnp.stack(lst) for lst in collected[gi]]
        outs += st[:4] + [rw_bufs[gi], st[4]]
    return (y_prompt, y_sample) + tuple(outs)
```
